```python
import math
import jax, jax.numpy as jnp
from jax import lax
import numpy as np

D_MODEL = 1024
BATCH = 8
SEQ = 8192
DEPTH = 1

MIX_WIDTH = D_MODEL
GM_WIDTH = MIX_WIDTH // 2
GM_HEADS = 4
GM_HEAD_DIM = GM_WIDTH // GM_HEADS
CHUNK = 128
SSM_WIDTH = MIX_WIDTH - GM_WIDTH
SSM_GROUP = 16
SSM_GROUPS = SSM_WIDTH // SSM_GROUP
SSM_STATE = 64
IN_WIDTH = 2 * GM_WIDTH + SSM_WIDTH
PEER_HEADS = 8
PEER_NKEYS = 128
PEER_EXPERTS = PEER_NKEYS * PEER_NKEYS
PEER_TOPK = 16
PEER_DKEY = 128
PEER_BLOCK = 128
N_MOD = 6
EPS = 1e-6

kernel_name = "hybrid_gmlp_s5_peer_adaln_block"


def _rmsnorm(x, g):
    xf = x.astype(jnp.float32)
    y = xf * lax.rsqrt(jnp.mean(xf * xf, axis=-1, keepdims=True) + EPS)
    return (y * g.astype(jnp.float32)).astype(x.dtype)


def _layernorm(x, g, b):
    xf = x.astype(jnp.float32)
    mu = jnp.mean(xf, axis=-1, keepdims=True)
    var = jnp.mean(jnp.square(xf - mu), axis=-1, keepdims=True)
    y = (xf - mu) * lax.rsqrt(var + EPS)
    return (y * g.astype(jnp.float32) + b.astype(jnp.float32)).astype(x.dtype)


def _sgu(zu, zv, ln_g, ln_b, w_s, b_s):
    u = jax.nn.gelu(zu)
    v = _layernorm(jax.nn.gelu(zv), ln_g, ln_b)
    bsz, seq, _ = v.shape
    vc = v.reshape(bsz, seq // CHUNK, CHUNK, GM_HEADS, GM_HEAD_DIM)
    mask = jnp.tril(jnp.ones((CHUNK, CHUNK), dtype=w_s.dtype))
    mixed = jnp.einsum('hij,bnjhd->bnihd', w_s * mask, vc)
    mixed = mixed + jnp.transpose(b_s)[:, :, None]
    return u * mixed.reshape(bsz, seq, GM_WIDTH)


def _ssm_combine(e1, e2):
    a1, b1 = e1
    a2, b2 = e2
    return a1 * a2, a2 * b1 + b2


def _s5(u, a_re, a_im, log_dt, b_re, b_im, c_re, c_im, d_skip, w_glu, b_glu):
    f32 = jnp.float32
    bsz, seq, _ = u.shape
    lam = lax.complex(jnp.minimum(a_re.astype(f32), -1e-4), a_im.astype(f32))
    delta = jnp.exp(log_dt.astype(f32))[:, None]
    a_bar = jnp.exp(lam * delta)
    b_mat = lax.complex(b_re.astype(f32), b_im.astype(f32))
    b_bar = ((a_bar - 1.0) / lam)[:, :, None] * b_mat
    c_mat = lax.complex(c_re.astype(f32), c_im.astype(f32))
    ug = u.astype(f32).reshape(bsz, seq, SSM_GROUPS, SSM_GROUP)

    def one_sequence(us):
        bu = jnp.einsum('gph,sgh->sgp', b_bar, us.astype(jnp.complex64))
        a = jnp.broadcast_to(a_bar, bu.shape)
        _, states = lax.associative_scan(_ssm_combine, (a, bu), axis=0)
        return jnp.einsum('ghp,sgp->sgh', c_mat, states).real

    y = lax.map(one_sequence, ug) + d_skip.astype(f32) * ug
    y = jax.nn.gelu(y.reshape(bsz, seq, SSM_WIDTH))
    y = y * jax.nn.sigmoid(y @ w_glu.astype(f32) + b_glu.astype(f32))
    return y.astype(u.dtype)


def _peer(h, w_q, keys, u_tab, v_tab):
    f32 = jnp.float32
    bsz, seq, dm = h.shape
    hb = h.reshape(-1, PEER_BLOCK, dm)

    def block(t):
        tb = t.shape[0]
        q = (t @ w_q).reshape(tb, PEER_HEADS, 2, PEER_DKEY // 2)
        s = jnp.einsum('thcd,hckd->thck', q.astype(f32), keys.astype(f32))
        top_s, top_i = lax.top_k(s, PEER_TOPK)
        cand_s = (top_s[:, :, 0, :, None] + top_s[:, :, 1, None, :]).reshape(tb, PEER_HEADS, -1)
        cand_i = (top_i[:, :, 0, :, None] * PEER_NKEYS + top_i[:, :, 1, None, :]).reshape(tb, PEER_HEADS, -1)
        best_s, pos = lax.top_k(cand_s, PEER_TOPK)
        idx = jnp.take_along_axis(cand_i, pos, axis=-1)
        g = jax.nn.softmax(best_s, axis=-1)
        act = jax.nn.gelu(jnp.einsum('thkd,td->thk', u_tab[idx], t).astype(f32))
        w = (g * act).astype(t.dtype)
        return jnp.einsum('thk,thkd->td', w, v_tab[idx])

    return lax.map(block, hb).reshape(bsz, seq, dm)


def setup_inputs(seed: int = 0) -> dict:
    key = jax.random.key(seed)
    ks = jax.random.split(key, 32)
    f32 = jnp.float32

    def nrm(k, shape, scale):
        return jax.random.normal(k, shape, f32) * scale

    L = DEPTH
    n_idx = jnp.arange(SSM_STATE, dtype=f32)
    a_im = jnp.broadcast_to(math.pi * n_idx, (L, SSM_GROUPS, SSM_STATE)) + nrm(ks[11], (L, SSM_GROUPS, SSM_STATE), 0.01)
    a_re = -0.5 + nrm(ks[10], (L, SSM_GROUPS, SSM_STATE), 0.01)
    log_dt = jax.random.uniform(ks[12], (L, SSM_GROUPS), f32, math.log(1e-3), math.log(1e-1))
    return {
        "x": nrm(ks[0], (BATCH, SEQ, D_MODEL), 1.0),
        "c": nrm(ks[1], (BATCH, D_MODEL), 1.0),
        "w_ada": nrm(ks[2], (L, D_MODEL, N_MOD * D_MODEL), 0.5 * D_MODEL ** -0.5),
        "b_ada": nrm(ks[3], (L, N_MOD * D_MODEL), 0.02),
        "g_mix": 1.0 + nrm(ks[4], (L, D_MODEL), 0.02),
        "w_in": nrm(ks[5], (L, D_MODEL, IN_WIDTH), D_MODEL ** -0.5),
        "sgu_ln_g": 1.0 + nrm(ks[6], (L, GM_WIDTH), 0.02),
        "sgu_ln_b": nrm(ks[7], (L, GM_WIDTH), 0.02),
        "w_s": nrm(ks[8], (L, GM_HEADS, CHUNK, CHUNK), CHUNK ** -0.5),
        "b_s": 1.0 + nrm(ks[9], (L, GM_HEADS, CHUNK), 0.02),
        "ssm_a_re": a_re,
        "ssm_a_im": a_im,
        "ssm_log_dt": log_dt,
        "ssm_b_re": nrm(ks[13], (L, SSM_GROUPS, SSM_STATE, SSM_GROUP), (2 * SSM_GROUP) ** -0.5),
        "ssm_b_im": nrm(ks[14], (L, SSM_GROUPS, SSM_STATE, SSM_GROUP), (2 * SSM_GROUP) ** -0.5),
        "ssm_c_re": nrm(ks[15], (L, SSM_GROUPS, SSM_GROUP, SSM_STATE), 2.0 * SSM_STATE ** -0.5),
        "ssm_c_im": nrm(ks[16], (L, SSM_GROUPS, SSM_GROUP, SSM_STATE), 2.0 * SSM_STATE ** -0.5),
        "ssm_d": nrm(ks[17], (L, SSM_GROUPS, SSM_GROUP), 0.5),
        "w_glu": nrm(ks[18], (L, SSM_WIDTH, SSM_WIDTH), SSM_WIDTH ** -0.5),
        "b_glu": nrm(ks[19], (L, SSM_WIDTH), 0.02),
        "w_out": nrm(ks[20], (L, MIX_WIDTH, D_MODEL), MIX_WIDTH ** -0.5),
        "g_ffn": 1.0 + nrm(ks[21], (L, D_MODEL), 0.02),
        "w_q": nrm(ks[22], (L, D_MODEL, PEER_HEADS * PEER_DKEY), D_MODEL ** -0.5),
        "peer_keys": nrm(ks[23], (L, PEER_HEADS, 2, PEER_NKEYS, PEER_DKEY // 2), (PEER_DKEY // 2) ** -0.5),
        "peer_u": nrm(ks[24], (L, PEER_EXPERTS, D_MODEL), D_MODEL ** -0.5),
        "peer_v": nrm(ks[25], (L, PEER_EXPERTS, D_MODEL), PEER_HEADS ** -0.5),
        "g_final": 1.0 + nrm(ks[26], (D_MODEL,), 0.02),
    }


def reference(x, c, w_ada, b_ada, g_mix, w_in, sgu_ln_g, sgu_ln_b, w_s, b_s,
              ssm_a_re, ssm_a_im, ssm_log_dt, ssm_b_re, ssm_b_im, ssm_c_re, ssm_c_im,
              ssm_d, w_glu, b_glu, w_out, g_ffn, w_q, peer_keys, peer_u, peer_v, g_final):
    cond = jax.nn.silu(c)
    for l in range(DEPTH):
        mod = cond @ w_ada[l] + b_ada[l]
        sh1, sc1, gt1, sh2, sc2, gt2 = jnp.split(mod[:, None, :], N_MOD, axis=-1)

        h = _rmsnorm(x, g_mix[l]) * (1.0 + sc1) + sh1
        z = h @ w_in[l]
        zu = z[..., :GM_WIDTH]
        zv = z[..., GM_WIDTH:2 * GM_WIDTH]
        zs = z[..., 2 * GM_WIDTH:]
        y_gm = _sgu(zu, zv, sgu_ln_g[l], sgu_ln_b[l], w_s[l], b_s[l])
        y_ssm = _s5(zs, ssm_a_re[l], ssm_a_im[l], ssm_log_dt[l], ssm_b_re[l], ssm_b_im[l],
                    ssm_c_re[l], ssm_c_im[l], ssm_d[l], w_glu[l], b_glu[l])
        y = jnp.concatenate([y_gm, y_ssm], axis=-1) @ w_out[l]
        x = x + gt1 * y

        h2 = _rmsnorm(x, g_ffn[l]) * (1.0 + sc2) + sh2
        x = x + gt2 * _peer(h2, w_q[l], peer_keys[l], peer_u[l], peer_v[l])
    return _rmsnorm(x, g_final)
```

```python
import functools
import math

import jax
import jax.numpy as jnp
from jax import lax
from jax.experimental import pallas as pl
from jax.experimental.pallas import tpu as pltpu

f32 = jnp.float32
bf16 = jnp.bfloat16
i32 = jnp.int32

EPS = 1e-6
LANES = 128
SUBLANES = 8
GM_HEADS = 4
CHUNK = 128
SSM_GROUPS = 32
SSM_GROUP = 16
SSM_STATE = 64
PEER_HEADS = 8
PEER_NKEYS = 128
PEER_TOPK = 16
PEER_DHALF = 64
N_MOD = 6
VMEM_LIMIT = 48 * 1024 * 1024

_HIGHEST = lax.Precision.HIGHEST
_NT = (((1,), (1,)), ((), ()))


def _gelu(x):
    return 0.5 * x * (1.0 + jnp.tanh(0.7978845608028654 * (x + 0.044715 * (x * x * x))))


def _rms(x, g):
    return x * lax.rsqrt(jnp.mean(x * x, axis=-1, keepdims=True) + EPS) * g


def _ada_kernel(c_ref, w_ref, b_ref, o_ref):
    c = c_ref[...]
    cond = c * jax.nn.sigmoid(c)
    o_ref[...] = jnp.dot(cond, w_ref[...], precision=_HIGHEST, preferred_element_type=f32) + b_ref[...]


def _ada(c, w, b):
    bsz, d = c.shape
    n = w.shape[1]
    tn = 1536
    return pl.pallas_call(
        _ada_kernel,
        grid=(n // tn,),
        in_specs=[pl.BlockSpec((bsz, d), lambda j: (0, 0)),
                  pl.BlockSpec((d, tn), lambda j: (0, j)),
                  pl.BlockSpec((1, tn), lambda j: (0, j))],
        out_specs=pl.BlockSpec((bsz, tn), lambda j: (0, j)),
        out_shape=jax.ShapeDtypeStruct((bsz, n), f32),
        compiler_params=pltpu.CompilerParams(vmem_limit_bytes=VMEM_LIMIT),
        name="ada",
    )(c, w, b.reshape(1, n))


def _s5_params_kernel(are_ref, aim_ref, ldt_ref, bre_ref, bim_ref,
                      abr_ref, abi_ref, bbr_ref, bbi_ref):
    lre = jnp.minimum(are_ref[...], -1e-4)
    lim = aim_ref[...]
    dt = jnp.exp(ldt_ref[...])
    mag = jnp.exp(lre * dt)
    abr = mag * jnp.cos(lim * dt)
    abi = mag * jnp.sin(lim * dt)
    abr_ref[...] = abr
    abi_ref[...] = abi
    nr = abr - 1.0
    ni = abi
    den = lre * lre + lim * lim
    fr = (nr * lre + ni * lim) / den
    fi = (ni * lre - nr * lim) / den
    bre = bre_ref[...]
    bim = bim_ref[...]
    bbr_ref[...] = fr * bre - fi * bim
    bbi_ref[...] = fr * bim + fi * bre


def _s5_params(a_re, a_im, log_dt, b_re, b_im):
    g, p = a_re.shape
    hg = b_re.shape[2]
    are = a_re.reshape(g, 1, p)
    aim = a_im.reshape(g, 1, p)
    ldt = jnp.broadcast_to(log_dt.reshape(g, 1, 1), (g, 1, p))
    bre = jnp.transpose(b_re, (0, 2, 1))
    bim = jnp.transpose(b_im, (0, 2, 1))
    small = jax.ShapeDtypeStruct((g, 1, p), f32)
    big = jax.ShapeDtypeStruct((g, hg, p), f32)
    return pl.pallas_call(
        _s5_params_kernel,
        out_shape=(small, small, big, big),
        name="s5_params",
    )(are, aim, ldt, bre, bim)


def _mix_in_kernel(x_ref, mod_ref, g_ref, win_ref, lng_ref, lnb_ref, ws_ref, bs_ref,
                   ygm_ref, zs_ref):
    tm = x_ref.shape[0]
    gw = ygm_ref.shape[1]
    x = x_ref[...]
    h = _rms(x, g_ref[...]) * (1.0 + mod_ref[1:2, :]) + mod_ref[0:1, :]
    z = jnp.dot(h.astype(bf16), win_ref[...], preferred_element_type=f32)
    zs_ref[...] = z[:, 2 * gw:]
    u = _gelu(z[:, :gw])
    gv = _gelu(z[:, gw:2 * gw])
    mu = jnp.mean(gv, axis=-1, keepdims=True)
    dv = gv - mu
    var = jnp.mean(dv * dv, axis=-1, keepdims=True)
    v = (dv * lax.rsqrt(var + EPS) * lng_ref[...] + lnb_ref[...]).astype(bf16)
    row = lax.broadcasted_iota(i32, (CHUNK, CHUNK), 0)
    col = lax.broadcasted_iota(i32, (CHUNK, CHUNK), 1)
    causal = row >= col
    hd_w = gw // GM_HEADS
    for hd in range(GM_HEADS):
        wm = jnp.where(causal, ws_ref[hd], 0.0).astype(bf16)
        bias = bs_ref[hd]
        for ck in range(tm // CHUNK):
            rs = slice(ck * CHUNK, (ck + 1) * CHUNK)
            cs = slice(hd * hd_w, (hd + 1) * hd_w)
            mixed = jnp.dot(wm, v[rs, cs], preferred_element_type=f32) + bias
            ygm_ref[rs, cs] = (u[rs, cs] * mixed).astype(bf16)


def _mix_in(x, mod3, g_mix, w_in, ln_g, ln_b, w_s, b_s, tm):
    bsz, seq, d = x.shape
    gw = ln_g.shape[0]
    nin = w_in.shape[1]
    bs_b = jnp.broadcast_to(b_s[:, :, None], (GM_HEADS, CHUNK, gw // GM_HEADS))
    return pl.pallas_call(
        _mix_in_kernel,
        grid=(bsz, seq // tm),
        in_specs=[pl.BlockSpec((None, tm, d), lambda b, i: (b, i, 0)),
                  pl.BlockSpec((None, N_MOD, d), lambda b, i: (b, 0, 0)),
                  pl.BlockSpec((1, d), lambda b, i: (0, 0)),
                  pl.BlockSpec((d, nin), lambda b, i: (0, 0)),
                  pl.BlockSpec((1, gw), lambda b, i: (0, 0)),
                  pl.BlockSpec((1, gw), lambda b, i: (0, 0)),
                  pl.BlockSpec((GM_HEADS, CHUNK, CHUNK), lambda b, i: (0, 0, 0)),
                  pl.BlockSpec((GM_HEADS, CHUNK, gw // GM_HEADS), lambda b, i: (0, 0, 0))],
        out_specs=[pl.BlockSpec((None, tm, gw), lambda b, i: (b, i, 0)),
                   pl.BlockSpec((None, tm, nin - 2 * gw), lambda b, i: (b, i, 0))],
        out_shape=(jax.ShapeDtypeStruct((bsz, seq, gw), bf16),
                   jax.ShapeDtypeStruct((bsz, seq, nin - 2 * gw), f32)),
        compiler_params=pltpu.CompilerParams(
            dimension_semantics=("parallel", "parallel"), vmem_limit_bytes=VMEM_LIMIT),
        name="mix_in",
    )(x, mod3, g_mix.reshape(1, d), w_in.astype(bf16), ln_g.reshape(1, gw), ln_b.reshape(1, gw), w_s, bs_b)


def _s5_kernel(zs_ref, wb_ref, av_ref, wc_ref, dsk_ref, wglu_ref, bglu_ref, o_ref,
               zi, st, carry, yo):
    bsz, tt, w = zs_ref.shape
    nlt = w // LANES
    hw = st.shape[2] // 2
    nhalf = st.shape[0]

    @pl.when(pl.program_id(0) == 0)
    def _():
        carry[...] = jnp.zeros_like(carry)

    for b in range(bsz):
        zb = zs_ref[b]
        for j in range(nlt):
            zi[j, pl.ds(b, tt, stride=bsz), :] = zb[:, j * LANES:(j + 1) * LANES]

    ys = []
    lt_per_half = nlt // nhalf
    for hf in range(nhalf):
        zh = jnp.concatenate([zi[hf * lt_per_half + j] for j in range(lt_per_half)], axis=1)
        st[hf] = jnp.dot(zh.astype(bf16), wb_ref[hf], preferred_element_type=f32)
        ar = jnp.broadcast_to(av_ref[2 * hf:2 * hf + 1, :], (bsz, hw))
        ai = jnp.broadcast_to(av_ref[2 * hf + 1:2 * hf + 2, :], (bsz, hw))

        def step(t, xr, xi):
            r0 = pl.multiple_of(t * bsz, bsz)
            bur = st[hf, pl.ds(r0, bsz), 0:hw]
            bui = st[hf, pl.ds(r0, bsz), hw:2 * hw]
            nxr = ar * xr - ai * xi + bur
            nxi = ar * xi + ai * xr + bui
            st[hf, pl.ds(r0, bsz), 0:hw] = nxr
            st[hf, pl.ds(r0, bsz), hw:2 * hw] = nxi
            return nxr, nxi

        def body(t2, c):
            xr, xi = c
            xr, xi = step(2 * t2, xr, xi)
            xr, xi = step(2 * t2 + 1, xr, xi)
            return xr, xi

        xr, xi = lax.fori_loop(0, tt // 2, body, (carry[hf, :, 0:hw], carry[hf, :, hw:2 * hw]))
        carry[hf, :, 0:hw] = xr
        carry[hf, :, hw:2 * hw] = xi
        ys.append(jnp.dot(st[hf].astype(bf16), wc_ref[hf], preferred_element_type=f32))

    u = jnp.concatenate([zi[j] for j in range(nlt)], axis=1)
    y = jnp.concatenate(ys, axis=1) + dsk_ref[...] * u
    y = _gelu(y)
    gate = jax.nn.sigmoid(jnp.dot(y.astype(bf16), wglu_ref[...], preferred_element_type=f32) + bglu_ref[...])
    y = y * gate
    for j in range(nlt):
        yo[j] = y[:, j * LANES:(j + 1) * LANES]
    for b in range(bsz):
        o_ref[b] = jnp.concatenate(
            [yo[j, pl.ds(b, tt, stride=bsz), :] for j in range(nlt)], axis=1).astype(bf16)


def _s5(zs, abr, abi, bbr, bbi, c_re, c_im, d_skip, w_glu, b_glu, tt):
    bsz, seq, w = zs.shape
    g, hg, p = bbr.shape
    nhalf = 2
    gh = g // nhalf
    eye = jnp.eye(gh, dtype=f32)

    def blockdiag_in(m):
        return jnp.einsum('ghp,gk->ghkp', m, eye).reshape(gh * hg, gh * p)

    def blockdiag_out(m):
        return jnp.einsum('ghp,gk->gpkh', m, eye).reshape(gh * p, gh * hg)

    wb = jnp.stack([jnp.concatenate([blockdiag_in(bbr[h * gh:(h + 1) * gh]),
                                     blockdiag_in(bbi[h * gh:(h + 1) * gh])], axis=1)
                    for h in range(nhalf)]).astype(bf16)
    wc = jnp.stack([jnp.concatenate([blockdiag_out(c_re[h * gh:(h + 1) * gh]),
                                     blockdiag_out(-c_im[h * gh:(h + 1) * gh])], axis=0)
                    for h in range(nhalf)]).astype(bf16)
    av = jnp.concatenate([jnp.stack([abr[h * gh:(h + 1) * gh].reshape(gh * p),
                                     abi[h * gh:(h + 1) * gh].reshape(gh * p)])
                          for h in range(nhalf)], axis=0)
    hw = gh * p
    nlt = w // LANES
    return pl.pallas_call(
        _s5_kernel,
        grid=(seq // tt,),
        in_specs=[pl.BlockSpec((bsz, tt, w), lambda i: (0, i, 0)),
                  pl.BlockSpec(wb.shape, lambda i: (0, 0, 0)),
                  pl.BlockSpec(av.shape, lambda i: (0, 0)),
                  pl.BlockSpec(wc.shape, lambda i: (0, 0, 0)),
                  pl.BlockSpec((1, w), lambda i: (0, 0)),
                  pl.BlockSpec((w, w), lambda i: (0, 0)),
                  pl.BlockSpec((1, w), lambda i: (0, 0))],
        out_specs=pl.BlockSpec((bsz, tt, w), lambda i: (0, i, 0)),
        out_shape=jax.ShapeDtypeStruct((bsz, seq, w), bf16),
        scratch_shapes=[pltpu.VMEM((nlt, tt * bsz, LANES), f32),
                        pltpu.VMEM((nhalf, tt * bsz, 2 * hw), f32),
                        pltpu.VMEM((nhalf, bsz, 2 * hw), f32),
                        pltpu.VMEM((nlt, tt * bsz, LANES), f32)],
        compiler_params=pltpu.CompilerParams(
            dimension_semantics=("arbitrary",), vmem_limit_bytes=VMEM_LIMIT),
        name="s5",
    )(zs, wb, av, wc, d_skip.reshape(1, w), w_glu.astype(bf16), b_glu.reshape(1, w))


def _top16(vals, rowid):
    nrow = vals.shape[0]
    tops, topi = [], []
    for _ in range(PEER_TOPK):
        m = jnp.max(vals, axis=0, keepdims=True)
        am = jnp.min(jnp.where(vals == m, rowid, float(nrow)), axis=0, keepdims=True)
        tops.append(m)
        topi.append(am)
        vals = jnp.where(rowid == am, -jnp.inf, vals)
    return tops, topi


def _mix_out_kernel(x_ref, ygm_ref, yss_ref, mod_ref, wo1_ref, wo2_ref, g_ref, wqt_ref, keys_ref,
                    x1_ref, h2_ref, idx_ref, gate_ref):
    tm, d = x_ref.shape
    y = jnp.dot(ygm_ref[...], wo1_ref[...], preferred_element_type=f32)
    y = y + jnp.dot(yss_ref[...], wo2_ref[...], preferred_element_type=f32)
    x1 = x_ref[...] + mod_ref[2:3, :] * y
    x1_ref[...] = x1
    h2 = _rms(x1, g_ref[...]) * (1.0 + mod_ref[4:5, :]) + mod_ref[3:4, :]
    for c in range(d // LANES):
        h2_ref[pl.ds(c, tm, stride=d // LANES), :] = h2[:, c * LANES:(c + 1) * LANES]
    qt = lax.dot_general(wqt_ref[...], h2.astype(bf16), _NT, preferred_element_type=f32).astype(bf16)

    keyid = lax.broadcasted_iota(i32, (PEER_NKEYS, tm), 0).astype(f32)
    pos_rows = [float(j) for j in range(16)]
    for i in range(1, 8):
        pos_rows += [float(i * 16 + j) for j in range(8)]
    pos_rows += [float(i * 16) for i in range(8, 16)]
    ncand = len(pos_rows)
    prow = lax.broadcasted_iota(i32, (ncand, tm), 0)
    pos = jnp.where(prow < 16, prow,
                    jnp.where(prow < 72, jnp.right_shift(prow - 8, 3) * 16 + jnp.bitwise_and(prow, 7),
                              (prow - 64) * 16)).astype(f32)

    idx_rows, gate_rows = [], []
    for hd in range(PEER_HEADS):
        halves = []
        for c in range(2):
            off = hd * 2 * PEER_DHALF + c * PEER_DHALF
            s = jnp.dot(keys_ref[hd, c], qt[off:off + PEER_DHALF, :], preferred_element_type=f32)
            halves.append(_top16(s, keyid))
        (ta, ia), (tb, ib) = halves
        b16 = jnp.concatenate(tb, axis=0)
        ib16 = jnp.concatenate(ib, axis=0)
        b8, ib8 = b16[:8], ib16[:8]
        cand = [ta[0] + b16]
        cid = [ia[0] * float(PEER_NKEYS) + ib16]
        for i in range(1, 8):
            cand.append(ta[i] + b8)
            cid.append(ia[i] * float(PEER_NKEYS) + ib8)
        a_hi = jnp.concatenate(ta[8:], axis=0)
        ia_hi = jnp.concatenate(ia[8:], axis=0)
        cand.append(a_hi + tb[0])
        cid.append(ia_hi * float(PEER_NKEYS) + ib[0])
        cand = jnp.concatenate(cand, axis=0)
        cid = jnp.concatenate(cid, axis=0)
        best, eid = [], []
        for _ in range(PEER_TOPK):
            m = jnp.max(cand, axis=0, keepdims=True)
            p = jnp.min(jnp.where(cand == m, pos, 1e9), axis=0, keepdims=True)
            sel = pos == p
            eid.append(jnp.max(jnp.where(sel, cid, -1.0), axis=0, keepdims=True))
            best.append(m)
            cand = jnp.where(sel, -jnp.inf, cand)
        best = jnp.concatenate(best, axis=0)
        e = jnp.exp(best - best[0:1])
        gate_rows.append(e / jnp.sum(e, axis=0, keepdims=True))
        idx_rows.append(jnp.concatenate(eid, axis=0))
    gate_t = jnp.concatenate(gate_rows, axis=0)
    idx_t = jnp.concatenate(idx_rows, axis=0)
    gate_ref[...] = gate_t.T
    idx_ref[...] = (idx_t.T * 4.0).astype(i32)


def _mix_out(x, ygm, yss, mod3, w_out, g_ffn, w_q, keys, tm):
    bsz, seq, d = x.shape
    gw = ygm.shape[2]
    nt = seq // tm
    nk = PEER_HEADS * PEER_TOPK
    n = bsz * seq
    wo = w_out.astype(bf16)
    return pl.pallas_call(
        _mix_out_kernel,
        grid=(bsz, nt),
        in_specs=[pl.BlockSpec((None, tm, d), lambda b, i: (b, i, 0)),
                  pl.BlockSpec((None, tm, gw), lambda b, i: (b, i, 0)),
                  pl.BlockSpec((None, tm, gw), lambda b, i: (b, i, 0)),
                  pl.BlockSpec((None, N_MOD, d), lambda b, i: (b, 0, 0)),
                  pl.BlockSpec((gw, d), lambda b, i: (0, 0)),
                  pl.BlockSpec((gw, d), lambda b, i: (0, 0)),
                  pl.BlockSpec((1, d), lambda b, i: (0, 0)),
                  pl.BlockSpec((d, d), lambda b, i: (0, 0)),
                  pl.BlockSpec(keys.shape, lambda b, i: (0, 0, 0, 0))],
        out_specs=[pl.BlockSpec((tm, d), lambda b, i: (b * nt + i, 0)),
                   pl.BlockSpec((tm * (d // LANES), LANES), lambda b, i: (b * nt + i, 0)),
                   pl.BlockSpec((tm, nk), lambda b, i: (b * nt + i, 0)),
                   pl.BlockSpec((tm, nk), lambda b, i: (b * nt + i, 0))],
        out_shape=(jax.ShapeDtypeStruct((n, d), f32),
                   jax.ShapeDtypeStruct((n * (d // LANES), LANES), f32),
                   jax.ShapeDtypeStruct((n, nk), i32),
                   jax.ShapeDtypeStruct((n, nk), f32)),
        compiler_params=pltpu.CompilerParams(
            dimension_semantics=("parallel", "parallel"), vmem_limit_bytes=VMEM_LIMIT),
        name="mix_out",
    )(x, ygm, yss, mod3, wo[:gw], wo[gw:], g_ffn.reshape(1, d), w_q.T.astype(bf16), keys.astype(bf16))


def _pack_table(tab):
    e, d = tab.shape
    tb = tab.astype(bf16).reshape(e, d // (2 * LANES), 2, LANES).transpose(0, 1, 3, 2)
    return lax.bitcast_convert_type(tb, i32).reshape(e * d // (2 * LANES), LANES)


def _gather_rows(idx_ref, t, tab_ref, stage, nk, rows):
    for k in range(nk):
        r = pl.multiple_of(idx_ref[t, k], rows)
        stage[rows * k:rows * (k + 1), :] = tab_ref[pl.ds(r, rows), :]
    return pltpu.bitcast(stage[...], bf16)


def _peer_u_kernel(idx_ref, h_ref, gate_ref, tab_ref, w_ref, stage, act):
    tt, nk = gate_ref.shape
    nch = h_ref.shape[1]
    rows = nch // 2
    r_i = lax.broadcasted_iota(i32, (LANES, LANES), 0)
    c_i = lax.broadcasted_iota(i32, (LANES, LANES), 1)
    ppt = LANES // nch
    eye = (r_i == c_i).astype(f32).reshape(ppt, nch, LANES)
    grp = (jnp.right_shift(r_i, int(math.log2(nch))) == jnp.bitwise_and(c_i, ppt - 1)).astype(f32)
    j_i = lax.broadcasted_iota(i32, (nk // ppt, nch, LANES), 0)
    n_i = lax.broadcasted_iota(i32, (nk // ppt, nch, LANES), 2)
    pick = (jnp.right_shift(n_i, int(math.log2(ppt))) == j_i).astype(f32).reshape(nk * nch // ppt, LANES)

    def body(t, carry):
        m = _gather_rows(idx_ref, t, tab_ref, stage, nk, rows)
        hrep = jnp.tile(h_ref[t], (ppt, 1)).astype(bf16)
        r = lax.dot_general(m, hrep, _NT, preferred_element_type=f32)
        a = (r.reshape(nk // ppt, ppt, nch, LANES) * eye).sum(axis=1)
        g = jnp.dot(a.reshape(-1, LANES), grp, precision=_HIGHEST, preferred_element_type=f32)
        act[pl.ds(t, 1), :] = jnp.sum(g * pick, axis=0, keepdims=True)
        return carry

    lax.fori_loop(0, tt, body, 0)
    w_ref[...] = gate_ref[...] * _gelu(act[...])


def _peer_v_kernel(idx_ref, w_ref, tab_ref, o_ref, stage, whi, wlo):
    tt, nk = w_ref.shape
    nch = o_ref.shape[1]
    rows = nch // 2
    d = nch * LANES
    k_i = lax.broadcasted_iota(i32, (nk, d), 0)
    n_i = lax.broadcasted_iota(i32, (nk, d), 1)
    sh = int(math.log2(nch))
    expand = (jnp.right_shift(n_i, sh) == k_i).astype(f32).astype(bf16)
    w = w_ref[...]
    hi = w.astype(bf16)
    lo = (w - hi.astype(f32)).astype(bf16)
    whi[...] = jnp.dot(hi, expand, preferred_element_type=f32)
    wlo[...] = jnp.dot(lo, expand, preferred_element_type=f32)
    c_i = lax.broadcasted_iota(i32, (nch, d), 0)
    l_i = lax.broadcasted_iota(i32, (nch, d), 1)
    diag = (jnp.bitwise_and(l_i, nch - 1) == c_i).astype(f32)

    def body(t, carry):
        m = _gather_rows(idx_ref, t, tab_ref, stage, nk, rows)
        lhs = jnp.concatenate([whi[pl.ds(t, 1), :] * diag, wlo[pl.ds(t, 1), :] * diag], axis=0).astype(bf16)
        out = jnp.dot(lhs, m, preferred_element_type=f32)
        o_ref[t] = out[:nch] + out[nch:]
        return carry

    lax.fori_loop(0, tt, body, 0)


def _peer_u(idx, h2r, gate, tab, tt):
    n, nk = gate.shape
    nch = h2r.shape[0] // n
    h3 = h2r.reshape(n, nch, LANES)
    return pl.pallas_call(
        _peer_u_kernel,
        grid=(n // tt,),
        in_specs=[pl.BlockSpec((tt, nk), lambda i: (i, 0), memory_space=pltpu.SMEM),
                  pl.BlockSpec((tt, nch, LANES), lambda i: (i, 0, 0)),
                  pl.BlockSpec((tt, nk), lambda i: (i, 0)),
                  pl.BlockSpec(tab.shape, lambda i: (0, 0), pipeline_mode=pl.Buffered(1))],
        out_specs=pl.BlockSpec((tt, nk), lambda i: (i, 0)),
        out_shape=jax.ShapeDtypeStruct((n, nk), f32),
        scratch_shapes=[pltpu.VMEM((nk * nch // 2, LANES), i32),
                        pltpu.VMEM((tt, nk), f32)],
        compiler_params=pltpu.CompilerParams(
            dimension_semantics=("parallel",), vmem_limit_bytes=VMEM_LIMIT),
        name="peer_u",
    )(idx, h3, gate, tab)


def _peer_v(idx, w, tab, nch, tt):
    n, nk = w.shape
    d = nch * LANES
    return pl.pallas_call(
        _peer_v_kernel,
        grid=(n // tt,),
        in_specs=[pl.BlockSpec((tt, nk), lambda i: (i, 0), memory_space=pltpu.SMEM),
                  pl.BlockSpec((tt, nk), lambda i: (i, 0)),
                  pl.BlockSpec(tab.shape, lambda i: (0, 0), pipeline_mode=pl.Buffered(1))],
        out_specs=pl.BlockSpec((tt, nch, LANES), lambda i: (i, 0, 0)),
        out_shape=jax.ShapeDtypeStruct((n, nch, LANES), f32),
        scratch_shapes=[pltpu.VMEM((nk * nch // 2, LANES), i32),
                        pltpu.VMEM((tt, d), f32),
                        pltpu.VMEM((tt, d), f32)],
        compiler_params=pltpu.CompilerParams(
            dimension_semantics=("parallel",), vmem_limit_bytes=VMEM_LIMIT),
        name="peer_v",
    )(idx, w, tab)


def _final_kernel(x1_ref, p_ref, mod_ref, g_ref, o_ref):
    tm, d = x1_ref.shape
    nch = d // LANES
    peer = jnp.concatenate([p_ref[pl.ds(c, tm, stride=nch), :] for c in range(nch)], axis=1)
    x2 = x1_ref[...] + mod_ref[5:6, :] * peer
    o_ref[...] = _rms(x2, g_ref[...])


def _final(x1, peer2d, mod3, g_final, bsz, seq, tm):
    n, d = x1.shape
    nt = seq // tm
    nch = d // LANES
    return pl.pallas_call(
        _final_kernel,
        grid=(bsz, nt),
        in_specs=[pl.BlockSpec((tm, d), lambda b, i: (b * nt + i, 0)),
                  pl.BlockSpec((tm * nch, LANES), lambda b, i: (b * nt + i, 0)),
                  pl.BlockSpec((None, N_MOD, d), lambda b, i: (b, 0, 0)),
                  pl.BlockSpec((1, d), lambda b, i: (0, 0))],
        out_specs=pl.BlockSpec((None, tm, d), lambda b, i: (b, i, 0)),
        out_shape=jax.ShapeDtypeStruct((bsz, seq, d), f32),
        compiler_params=pltpu.CompilerParams(
            dimension_semantics=("parallel", "parallel"), vmem_limit_bytes=VMEM_LIMIT),
        name="final",
    )(x1, peer2d, mod3, g_final.reshape(1, d))


def kernel(x, c, w_ada, b_ada, g_mix, w_in, sgu_ln_g, sgu_ln_b, w_s, b_s, ssm_a_re, ssm_a_im, ssm_log_dt, ssm_b_re, ssm_b_im, ssm_c_re, ssm_c_im, ssm_d, w_glu, b_glu, w_out, g_ffn, w_q, peer_keys, peer_u, peer_v, g_final):
    bsz, seq, d = x.shape
    depth = w_ada.shape[0]
    assert depth == 1 and d % LANES == 0 and seq % CHUNK == 0
    tm_in = min(seq, 512)
    tm_out = min(seq, 256)
    tt_scan = min(seq, 64)
    tt_peer = 128
    n = bsz * seq
    nch = d // LANES

    l = 0
    mod3 = _ada(c, w_ada[l], b_ada[l]).reshape(bsz, N_MOD, d)
    ygm, zs = _mix_in(x, mod3, g_mix[l], w_in[l], sgu_ln_g[l], sgu_ln_b[l], w_s[l], b_s[l], tm_in)
    abr, abi, bbr, bbi = _s5_params(ssm_a_re[l], ssm_a_im[l], ssm_log_dt[l], ssm_b_re[l], ssm_b_im[l])
    yss = _s5(zs, abr[:, 0, :], abi[:, 0, :], bbr, bbi, ssm_c_re[l], ssm_c_im[l],
              ssm_d[l], w_glu[l], b_glu[l], tt_scan)
    x1, h2r, idx, gate = _mix_out(x, ygm, yss, mod3, w_out[l], g_ffn[l], w_q[l], peer_keys[l], tm_out)
    w = _peer_u(idx, h2r, gate, _pack_table(peer_u[l]), tt_peer)
    peer = _peer_v(idx, w, _pack_table(peer_v[l]), nch, tt_peer)
    return _final(x1, peer.reshape(n * nch, LANES), mod3, g_final, bsz, seq, tm_in)
```

```python
import functools
import math

import jax
import jax.numpy as jnp
from jax import lax
from jax.experimental import pallas as pl
from jax.experimental.pallas import tpu as pltpu

f32 = jnp.float32
bf16 = jnp.bfloat16
i32 = jnp.int32

EPS = 1e-6
LANES = 128
SUBLANES = 8
GM_HEADS = 4
CHUNK = 128
SSM_GROUPS = 32
SSM_GROUP = 16
SSM_STATE = 64
PEER_HEADS = 8
PEER_NKEYS = 128
PEER_TOPK = 16
PEER_DHALF = 64
N_MOD = 6
VMEM_LIMIT = 48 * 1024 * 1024
PEER_UNROLL = 4

_HIGHEST = lax.Precision.HIGHEST
_NT = (((1,), (1,)), ((), ()))


def _gelu(x):
    return 0.5 * x * (1.0 + jnp.tanh(0.7978845608028654 * (x + 0.044715 * (x * x * x))))


def _rms(x, g):
    return x * lax.rsqrt(jnp.mean(x * x, axis=-1, keepdims=True) + EPS) * g


def _ada_kernel(c_ref, w_ref, b_ref, o_ref):
    c = c_ref[...]
    cond = c * jax.nn.sigmoid(c)
    o_ref[...] = jnp.dot(cond, w_ref[...], precision=_HIGHEST, preferred_element_type=f32) + b_ref[...]


def _ada(c, w, b):
    bsz, d = c.shape
    n = w.shape[1]
    tn = 1536
    return pl.pallas_call(
        _ada_kernel,
        grid=(n // tn,),
        in_specs=[pl.BlockSpec((bsz, d), lambda j: (0, 0)),
                  pl.BlockSpec((d, tn), lambda j: (0, j)),
                  pl.BlockSpec((1, tn), lambda j: (0, j))],
        out_specs=pl.BlockSpec((bsz, tn), lambda j: (0, j)),
        out_shape=jax.ShapeDtypeStruct((bsz, n), f32),
        compiler_params=pltpu.CompilerParams(vmem_limit_bytes=VMEM_LIMIT),
        name="ada",
    )(c, w, b.reshape(1, n))


def _s5_params_kernel(are_ref, aim_ref, ldt_ref, bre_ref, bim_ref,
                      abr_ref, abi_ref, bbr_ref, bbi_ref):
    lre = jnp.minimum(are_ref[...], -1e-4)
    lim = aim_ref[...]
    dt = jnp.exp(ldt_ref[...])
    mag = jnp.exp(lre * dt)
    abr = mag * jnp.cos(lim * dt)
    abi = mag * jnp.sin(lim * dt)
    abr_ref[...] = abr
    abi_ref[...] = abi
    nr = abr - 1.0
    ni = abi
    den = lre * lre + lim * lim
    fr = (nr * lre + ni * lim) / den
    fi = (ni * lre - nr * lim) / den
    bre = bre_ref[...]
    bim = bim_ref[...]
    bbr_ref[...] = fr * bre - fi * bim
    bbi_ref[...] = fr * bim + fi * bre


def _s5_params(a_re, a_im, log_dt, b_re, b_im):
    g, p = a_re.shape
    hg = b_re.shape[2]
    are = a_re.reshape(g, 1, p)
    aim = a_im.reshape(g, 1, p)
    ldt = jnp.broadcast_to(log_dt.reshape(g, 1, 1), (g, 1, p))
    bre = jnp.transpose(b_re, (0, 2, 1))
    bim = jnp.transpose(b_im, (0, 2, 1))
    small = jax.ShapeDtypeStruct((g, 1, p), f32)
    big = jax.ShapeDtypeStruct((g, hg, p), f32)
    return pl.pallas_call(
        _s5_params_kernel,
        out_shape=(small, small, big, big),
        name="s5_params",
    )(are, aim, ldt, bre, bim)


def _mix_in_kernel(x_ref, mod_ref, g_ref, win_ref, lng_ref, lnb_ref, ws_ref, bs_ref,
                   ygm_ref, zs_ref):
    tm = x_ref.shape[0]
    gw = ygm_ref.shape[1]
    x = x_ref[...]
    h = _rms(x, g_ref[...]) * (1.0 + mod_ref[1:2, :]) + mod_ref[0:1, :]
    z = jnp.dot(h.astype(bf16), win_ref[...], preferred_element_type=f32)
    zs_ref[...] = z[:, 2 * gw:]
    u = _gelu(z[:, :gw])
    gv = _gelu(z[:, gw:2 * gw])
    mu = jnp.mean(gv, axis=-1, keepdims=True)
    dv = gv - mu
    var = jnp.mean(dv * dv, axis=-1, keepdims=True)
    v = (dv * lax.rsqrt(var + EPS) * lng_ref[...] + lnb_ref[...]).astype(bf16)
    row = lax.broadcasted_iota(i32, (CHUNK, CHUNK), 0)
    col = lax.broadcasted_iota(i32, (CHUNK, CHUNK), 1)
    causal = row >= col
    hd_w = gw // GM_HEADS
    for hd in range(GM_HEADS):
        wm = jnp.where(causal, ws_ref[hd], 0.0).astype(bf16)
        bias = bs_ref[hd]
        for ck in range(tm // CHUNK):
            rs = slice(ck * CHUNK, (ck + 1) * CHUNK)
            cs = slice(hd * hd_w, (hd + 1) * hd_w)
            mixed = jnp.dot(wm, v[rs, cs], preferred_element_type=f32) + bias
            ygm_ref[rs, cs] = (u[rs, cs] * mixed).astype(bf16)


def _mix_in(x, mod3, g_mix, w_in, ln_g, ln_b, w_s, b_s, tm):
    bsz, seq, d = x.shape
    gw = ln_g.shape[0]
    nin = w_in.shape[1]
    bs_b = jnp.broadcast_to(b_s[:, :, None], (GM_HEADS, CHUNK, gw // GM_HEADS))
    return pl.pallas_call(
        _mix_in_kernel,
        grid=(bsz, seq // tm),
        in_specs=[pl.BlockSpec((None, tm, d), lambda b, i: (b, i, 0)),
                  pl.BlockSpec((None, N_MOD, d), lambda b, i: (b, 0, 0)),
                  pl.BlockSpec((1, d), lambda b, i: (0, 0)),
                  pl.BlockSpec((d, nin), lambda b, i: (0, 0)),
                  pl.BlockSpec((1, gw), lambda b, i: (0, 0)),
                  pl.BlockSpec((1, gw), lambda b, i: (0, 0)),
                  pl.BlockSpec((GM_HEADS, CHUNK, CHUNK), lambda b, i: (0, 0, 0)),
                  pl.BlockSpec((GM_HEADS, CHUNK, gw // GM_HEADS), lambda b, i: (0, 0, 0))],
        out_specs=[pl.BlockSpec((None, tm, gw), lambda b, i: (b, i, 0)),
                   pl.BlockSpec((None, tm, nin - 2 * gw), lambda b, i: (b, i, 0))],
        out_shape=(jax.ShapeDtypeStruct((bsz, seq, gw), bf16),
                   jax.ShapeDtypeStruct((bsz, seq, nin - 2 * gw), f32)),
        compiler_params=pltpu.CompilerParams(
            dimension_semantics=("parallel", "parallel"), vmem_limit_bytes=VMEM_LIMIT),
        name="mix_in",
    )(x, mod3, g_mix.reshape(1, d), w_in.astype(bf16), ln_g.reshape(1, gw), ln_b.reshape(1, gw), w_s, bs_b)


def _s5_kernel(zs_ref, wb_ref, av_ref, wc_ref, dsk_ref, wglu_ref, bglu_ref, o_ref,
               zi, st, carry, yo):
    bsz, tt, w = zs_ref.shape
    nlt = w // LANES
    hw = st.shape[2] // 2
    nhalf = st.shape[0]

    @pl.when(pl.program_id(0) == 0)
    def _():
        carry[...] = jnp.zeros_like(carry)

    for b in range(bsz):
        zb = zs_ref[b]
        for j in range(nlt):
            zi[j, pl.ds(b, tt, stride=bsz), :] = zb[:, j * LANES:(j + 1) * LANES]

    ys = []
    lt_per_half = nlt // nhalf
    for hf in range(nhalf):
        zh = jnp.concatenate([zi[hf * lt_per_half + j] for j in range(lt_per_half)], axis=1)
        st[hf] = jnp.dot(zh.astype(bf16), wb_ref[hf], preferred_element_type=f32)
        ar = jnp.broadcast_to(av_ref[2 * hf:2 * hf + 1, :], (bsz, hw))
        ai = jnp.broadcast_to(av_ref[2 * hf + 1:2 * hf + 2, :], (bsz, hw))

        def step(t, xr, xi):
            r0 = pl.multiple_of(t * bsz, bsz)
            bur = st[hf, pl.ds(r0, bsz), 0:hw]
            bui = st[hf, pl.ds(r0, bsz), hw:2 * hw]
            nxr = ar * xr - ai * xi + bur
            nxi = ar * xi + ai * xr + bui
            st[hf, pl.ds(r0, bsz), 0:hw] = nxr
            st[hf, pl.ds(r0, bsz), hw:2 * hw] = nxi
            return nxr, nxi

        def body(t2, c):
            xr, xi = c
            xr, xi = step(2 * t2, xr, xi)
            xr, xi = step(2 * t2 + 1, xr, xi)
            return xr, xi

        xr, xi = lax.fori_loop(0, tt // 2, body, (carry[hf, :, 0:hw], carry[hf, :, hw:2 * hw]))
        carry[hf, :, 0:hw] = xr
        carry[hf, :, hw:2 * hw] = xi
        ys.append(jnp.dot(st[hf].astype(bf16), wc_ref[hf], preferred_element_type=f32))

    u = jnp.concatenate([zi[j] for j in range(nlt)], axis=1)
    y = jnp.concatenate(ys, axis=1) + dsk_ref[...] * u
    y = _gelu(y)
    gate = jax.nn.sigmoid(jnp.dot(y.astype(bf16), wglu_ref[...], preferred_element_type=f32) + bglu_ref[...])
    y = y * gate
    for j in range(nlt):
        yo[j] = y[:, j * LANES:(j + 1) * LANES]
    for b in range(bsz):
        o_ref[b] = jnp.concatenate(
            [yo[j, pl.ds(b, tt, stride=bsz), :] for j in range(nlt)], axis=1).astype(bf16)


def _s5(zs, abr, abi, bbr, bbi, c_re, c_im, d_skip, w_glu, b_glu, tt):
    bsz, seq, w = zs.shape
    g, hg, p = bbr.shape
    nhalf = 2
    gh = g // nhalf
    eye = jnp.eye(gh, dtype=f32)

    def blockdiag_in(m):
        return jnp.einsum('ghp,gk->ghkp', m, eye).reshape(gh * hg, gh * p)

    def blockdiag_out(m):
        return jnp.einsum('ghp,gk->gpkh', m, eye).reshape(gh * p, gh * hg)

    wb = jnp.stack([jnp.concatenate([blockdiag_in(bbr[h * gh:(h + 1) * gh]),
                                     blockdiag_in(bbi[h * gh:(h + 1) * gh])], axis=1)
                    for h in range(nhalf)]).astype(bf16)
    wc = jnp.stack([jnp.concatenate([blockdiag_out(c_re[h * gh:(h + 1) * gh]),
                                     blockdiag_out(-c_im[h * gh:(h + 1) * gh])], axis=0)
                    for h in range(nhalf)]).astype(bf16)
    av = jnp.concatenate([jnp.stack([abr[h * gh:(h + 1) * gh].reshape(gh * p),
                                     abi[h * gh:(h + 1) * gh].reshape(gh * p)])
                          for h in range(nhalf)], axis=0)
    hw = gh * p
    nlt = w // LANES
    return pl.pallas_call(
        _s5_kernel,
        grid=(seq // tt,),
        in_specs=[pl.BlockSpec((bsz, tt, w), lambda i: (0, i, 0)),
                  pl.BlockSpec(wb.shape, lambda i: (0, 0, 0)),
                  pl.BlockSpec(av.shape, lambda i: (0, 0)),
                  pl.BlockSpec(wc.shape, lambda i: (0, 0, 0)),
                  pl.BlockSpec((1, w), lambda i: (0, 0)),
                  pl.BlockSpec((w, w), lambda i: (0, 0)),
                  pl.BlockSpec((1, w), lambda i: (0, 0))],
        out_specs=pl.BlockSpec((bsz, tt, w), lambda i: (0, i, 0)),
        out_shape=jax.ShapeDtypeStruct((bsz, seq, w), bf16),
        scratch_shapes=[pltpu.VMEM((nlt, tt * bsz, LANES), f32),
                        pltpu.VMEM((nhalf, tt * bsz, 2 * hw), f32),
                        pltpu.VMEM((nhalf, bsz, 2 * hw), f32),
                        pltpu.VMEM((nlt, tt * bsz, LANES), f32)],
        compiler_params=pltpu.CompilerParams(
            dimension_semantics=("arbitrary",), vmem_limit_bytes=VMEM_LIMIT),
        name="s5",
    )(zs, wb, av, wc, d_skip.reshape(1, w), w_glu.astype(bf16), b_glu.reshape(1, w))


def _top16(vals, rowid):
    nrow = vals.shape[0]
    tops, topi = [], []
    for _ in range(PEER_TOPK):
        m = jnp.max(vals, axis=0, keepdims=True)
        am = jnp.min(jnp.where(vals == m, rowid, float(nrow)), axis=0, keepdims=True)
        tops.append(m)
        topi.append(am)
        vals = jnp.where(rowid == am, -jnp.inf, vals)
    return tops, topi


def _mix_out_kernel(x_ref, ygm_ref, yss_ref, mod_ref, wo1_ref, wo2_ref, g_ref, wqt_ref, keys_ref,
                    x1_ref, h2_ref, idx_ref, gate_ref):
    tm, d = x_ref.shape
    y = jnp.dot(ygm_ref[...], wo1_ref[...], preferred_element_type=f32)
    y = y + jnp.dot(yss_ref[...], wo2_ref[...], preferred_element_type=f32)
    x1 = x_ref[...] + mod_ref[2:3, :] * y
    x1_ref[...] = x1
    h2 = _rms(x1, g_ref[...]) * (1.0 + mod_ref[4:5, :]) + mod_ref[3:4, :]
    for c in range(d // LANES):
        h2_ref[pl.ds(c, tm, stride=d // LANES), :] = h2[:, c * LANES:(c + 1) * LANES]
    qt = lax.dot_general(wqt_ref[...], h2.astype(bf16), _NT, preferred_element_type=f32).astype(bf16)

    keyid = lax.broadcasted_iota(i32, (PEER_NKEYS, tm), 0).astype(f32)
    pos_rows = [float(j) for j in range(16)]
    for i in range(1, 8):
        pos_rows += [float(i * 16 + j) for j in range(8)]
    pos_rows += [float(i * 16) for i in range(8, 16)]
    ncand = len(pos_rows)
    prow = lax.broadcasted_iota(i32, (ncand, tm), 0)
    pos = jnp.where(prow < 16, prow,
                    jnp.where(prow < 72, jnp.right_shift(prow - 8, 3) * 16 + jnp.bitwise_and(prow, 7),
                              (prow - 64) * 16)).astype(f32)

    idx_rows, gate_rows = [], []
    for hd in range(PEER_HEADS):
        halves = []
        for c in range(2):
            off = hd * 2 * PEER_DHALF + c * PEER_DHALF
            s = jnp.dot(keys_ref[hd, c], qt[off:off + PEER_DHALF, :], preferred_element_type=f32)
            halves.append(_top16(s, keyid))
        (ta, ia), (tb, ib) = halves
        b16 = jnp.concatenate(tb, axis=0)
        ib16 = jnp.concatenate(ib, axis=0)
        b8, ib8 = b16[:8], ib16[:8]
        cand = [ta[0] + b16]
        cid = [ia[0] * float(PEER_NKEYS) + ib16]
        for i in range(1, 8):
            cand.append(ta[i] + b8)
            cid.append(ia[i] * float(PEER_NKEYS) + ib8)
        a_hi = jnp.concatenate(ta[8:], axis=0)
        ia_hi = jnp.concatenate(ia[8:], axis=0)
        cand.append(a_hi + tb[0])
        cid.append(ia_hi * float(PEER_NKEYS) + ib[0])
        cand = jnp.concatenate(cand, axis=0)
        cid = jnp.concatenate(cid, axis=0)
        best, eid = [], []
        for _ in range(PEER_TOPK):
            m = jnp.max(cand, axis=0, keepdims=True)
            p = jnp.min(jnp.where(cand == m, pos, 1e9), axis=0, keepdims=True)
            sel = pos == p
            eid.append(jnp.max(jnp.where(sel, cid, -1.0), axis=0, keepdims=True))
            best.append(m)
            cand = jnp.where(sel, -jnp.inf, cand)
        best = jnp.concatenate(best, axis=0)
        e = jnp.exp(best - best[0:1])
        gate_rows.append(e / jnp.sum(e, axis=0, keepdims=True))
        idx_rows.append(jnp.concatenate(eid, axis=0))
    gate_t = jnp.concatenate(gate_rows, axis=0)
    idx_t = jnp.concatenate(idx_rows, axis=0)
    gate_ref[...] = gate_t.T
    idx_ref[...] = (idx_t.T * 4.0).astype(i32)


def _mix_out(x, ygm, yss, mod3, w_out, g_ffn, w_q, keys, tm):
    bsz, seq, d = x.shape
    gw = ygm.shape[2]
    nt = seq // tm
    nk = PEER_HEADS * PEER_TOPK
    n = bsz * seq
    wo = w_out.astype(bf16)
    return pl.pallas_call(
        _mix_out_kernel,
        grid=(bsz, nt),
        in_specs=[pl.BlockSpec((None, tm, d), lambda b, i: (b, i, 0)),
                  pl.BlockSpec((None, tm, gw), lambda b, i: (b, i, 0)),
                  pl.BlockSpec((None, tm, gw), lambda b, i: (b, i, 0)),
                  pl.BlockSpec((None, N_MOD, d), lambda b, i: (b, 0, 0)),
                  pl.BlockSpec((gw, d), lambda b, i: (0, 0)),
                  pl.BlockSpec((gw, d), lambda b, i: (0, 0)),
                  pl.BlockSpec((1, d), lambda b, i: (0, 0)),
                  pl.BlockSpec((d, d), lambda b, i: (0, 0)),
                  pl.BlockSpec(keys.shape, lambda b, i: (0, 0, 0, 0))],
        out_specs=[pl.BlockSpec((tm, d), lambda b, i: (b * nt + i, 0)),
                   pl.BlockSpec((tm * (d // LANES), LANES), lambda b, i: (b * nt + i, 0)),
                   pl.BlockSpec((tm, nk), lambda b, i: (b * nt + i, 0)),
                   pl.BlockSpec((tm, nk), lambda b, i: (b * nt + i, 0))],
        out_shape=(jax.ShapeDtypeStruct((n, d), f32),
                   jax.ShapeDtypeStruct((n * (d // LANES), LANES), f32),
                   jax.ShapeDtypeStruct((n, nk), i32),
                   jax.ShapeDtypeStruct((n, nk), f32)),
        compiler_params=pltpu.CompilerParams(
            dimension_semantics=("parallel", "parallel"), vmem_limit_bytes=VMEM_LIMIT),
        name="mix_out",
    )(x, ygm, yss, mod3, wo[:gw], wo[gw:], g_ffn.reshape(1, d), w_q.T.astype(bf16), keys.astype(bf16))


def _pack_table(tab):
    e, d = tab.shape
    tb = tab.astype(bf16).reshape(e, d // (2 * LANES), 2, LANES).transpose(0, 1, 3, 2)
    return lax.bitcast_convert_type(tb, i32).reshape(e * d // (2 * LANES), LANES)


def _gather_rows(idx_ref, ts, tab_ref, stages, nk, rows):
    row_refs = [idx_ref.at[pl.ds(t, 1)] for t in ts]
    for k in range(nk):
        for row_ref, stage in zip(row_refs, stages):
            r = pl.multiple_of(row_ref[0, k], rows)
            stage[rows * k:rows * (k + 1), :] = tab_ref[pl.ds(r, rows), :]
    return [pltpu.bitcast(stage[...], bf16) for stage in stages]


def _split_bf16(x):
    hi = x.astype(bf16)
    return hi, (x - hi.astype(f32)).astype(bf16)


def _peer_u_kernel(idx_ref, h_ref, gate_ref, tab_ref, w_ref, *scratch):
    stages, res_all = scratch[:-1], scratch[-1]
    tt, nk = gate_ref.shape
    nch = h_ref.shape[1]
    rows = nch // 2
    d = nch * LANES
    sh = int(math.log2(nch))
    nun = len(stages)

    def body(i, carry):
        ts = [i * nun + u for u in range(nun)]
        ms = _gather_rows(idx_ref, ts, tab_ref, stages, nk, rows)
        for t, m in zip(ts, ms):
            hb = h_ref[t].astype(bf16)
            r0 = pl.multiple_of(t * nch, nch)
            res_all[pl.ds(r0, nch), :] = lax.dot_general(hb, m, _NT, preferred_element_type=f32)
        return carry

    lax.fori_loop(0, tt // nun, body, 0)

    c_i = lax.broadcasted_iota(i32, (nch, d), 0)
    l_i = lax.broadcasted_iota(i32, (nch, d), 1)
    diag = (jnp.bitwise_and(l_i, nch - 1) == c_i).astype(f32)
    masked = (res_all[...].reshape(tt, nch, d) * diag).reshape(tt * nch, d)
    grp = (jnp.right_shift(lax.broadcasted_iota(i32, (d, nk), 0), sh)
           == lax.broadcasted_iota(i32, (d, nk), 1)).astype(f32).astype(bf16)
    mhi, mlo = _split_bf16(masked)
    g = jnp.dot(mhi, grp, preferred_element_type=f32) + jnp.dot(mlo, grp, preferred_element_type=f32)
    sel = (jnp.right_shift(lax.broadcasted_iota(i32, (tt, tt * nch), 1), sh)
           == lax.broadcasted_iota(i32, (tt, tt * nch), 0)).astype(f32).astype(bf16)
    ghi, glo = _split_bf16(g)
    act = jnp.dot(sel, ghi, preferred_element_type=f32) + jnp.dot(sel, glo, preferred_element_type=f32)
    w_ref[...] = gate_ref[...] * _gelu(act)


def _peer_v_kernel(idx_ref, w_ref, tab_ref, o_ref, *scratch):
    stages, whi, wlo = scratch[:-2], scratch[-2], scratch[-1]
    tt, nk = w_ref.shape
    nch = o_ref.shape[1]
    rows = nch // 2
    d = nch * LANES
    sh = int(math.log2(nch))
    nun = len(stages)
    k_i = lax.broadcasted_iota(i32, (nk, d), 0)
    n_i = lax.broadcasted_iota(i32, (nk, d), 1)
    expand = (jnp.right_shift(n_i, sh) == k_i).astype(f32).astype(bf16)
    hi, lo = _split_bf16(w_ref[...])
    whi[...] = jnp.dot(hi, expand, preferred_element_type=f32)
    wlo[...] = jnp.dot(lo, expand, preferred_element_type=f32)
    c_i = lax.broadcasted_iota(i32, (nch, d), 0)
    l_i = lax.broadcasted_iota(i32, (nch, d), 1)
    diag = (jnp.bitwise_and(l_i, nch - 1) == c_i).astype(f32)

    def body(i, carry):
        ts = [i * nun + u for u in range(nun)]
        ms = _gather_rows(idx_ref, ts, tab_ref, stages, nk, rows)
        for t, m in zip(ts, ms):
            lhs = jnp.concatenate([whi[pl.ds(t, 1), :] * diag, wlo[pl.ds(t, 1), :] * diag], axis=0).astype(bf16)
            out = jnp.dot(lhs, m, preferred_element_type=f32)
            o_ref[t] = out[:nch] + out[nch:]
        return carry

    lax.fori_loop(0, tt // nun, body, 0)


def _peer_u(idx, h2r, gate, tab, tt):
    n, nk = gate.shape
    nch = h2r.shape[0] // n
    h3 = h2r.reshape(n, nch, LANES)
    return pl.pallas_call(
        _peer_u_kernel,
        grid=(n // tt,),
        in_specs=[pl.BlockSpec((tt, nk), lambda i: (i, 0), memory_space=pltpu.SMEM),
                  pl.BlockSpec((tt, nch, LANES), lambda i: (i, 0, 0)),
                  pl.BlockSpec((tt, nk), lambda i: (i, 0)),
                  pl.BlockSpec(tab.shape, lambda i: (0, 0), pipeline_mode=pl.Buffered(1))],
        out_specs=pl.BlockSpec((tt, nk), lambda i: (i, 0)),
        out_shape=jax.ShapeDtypeStruct((n, nk), f32),
        scratch_shapes=[pltpu.VMEM((nk * nch // 2, LANES), i32)] * PEER_UNROLL
        + [pltpu.VMEM((tt * nch, nch * LANES), f32)],
        compiler_params=pltpu.CompilerParams(
            dimension_semantics=("parallel",), vmem_limit_bytes=VMEM_LIMIT),
        name="peer_u",
    )(idx, h3, gate, tab)


def _peer_v(idx, w, tab, nch, tt):
    n, nk = w.shape
    d = nch * LANES
    return pl.pallas_call(
        _peer_v_kernel,
        grid=(n // tt,),
        in_specs=[pl.BlockSpec((tt, nk), lambda i: (i, 0), memory_space=pltpu.SMEM),
                  pl.BlockSpec((tt, nk), lambda i: (i, 0)),
                  pl.BlockSpec(tab.shape, lambda i: (0, 0), pipeline_mode=pl.Buffered(1))],
        out_specs=pl.BlockSpec((tt, nch, LANES), lambda i: (i, 0, 0)),
        out_shape=jax.ShapeDtypeStruct((n, nch, LANES), f32),
        scratch_shapes=[pltpu.VMEM((nk * nch // 2, LANES), i32)] * PEER_UNROLL
        + [pltpu.VMEM((tt, d), f32), pltpu.VMEM((tt, d), f32)],
        compiler_params=pltpu.CompilerParams(
            dimension_semantics=("parallel",), vmem_limit_bytes=VMEM_LIMIT),
        name="peer_v",
    )(idx, w, tab)


def _final_kernel(x1_ref, p_ref, mod_ref, g_ref, o_ref):
    tm, d = x1_ref.shape
    nch = d // LANES
    peer = jnp.concatenate([p_ref[pl.ds(c, tm, stride=nch), :] for c in range(nch)], axis=1)
    x2 = x1_ref[...] + mod_ref[5:6, :] * peer
    o_ref[...] = _rms(x2, g_ref[...])


def _final(x1, peer2d, mod3, g_final, bsz, seq, tm):
    n, d = x1.shape
    nt = seq // tm
    nch = d // LANES
    return pl.pallas_call(
        _final_kernel,
        grid=(bsz, nt),
        in_specs=[pl.BlockSpec((tm, d), lambda b, i: (b * nt + i, 0)),
                  pl.BlockSpec((tm * nch, LANES), lambda b, i: (b * nt + i, 0)),
                  pl.BlockSpec((None, N_MOD, d), lambda b, i: (b, 0, 0)),
                  pl.BlockSpec((1, d), lambda b, i: (0, 0))],
        out_specs=pl.BlockSpec((None, tm, d), lambda b, i: (b, i, 0)),
        out_shape=jax.ShapeDtypeStruct((bsz, seq, d), f32),
        compiler_params=pltpu.CompilerParams(
            dimension_semantics=("parallel", "parallel"), vmem_limit_bytes=VMEM_LIMIT),
        name="final",
    )(x1, peer2d, mod3, g_final.reshape(1, d))


def kernel(x, c, w_ada, b_ada, g_mix, w_in, sgu_ln_g, sgu_ln_b, w_s, b_s, ssm_a_re, ssm_a_im, ssm_log_dt, ssm_b_re, ssm_b_im, ssm_c_re, ssm_c_im, ssm_d, w_glu, b_glu, w_out, g_ffn, w_q, peer_keys, peer_u, peer_v, g_final):
    bsz, seq, d = x.shape
    depth = w_ada.shape[0]
    assert depth == 1 and d % LANES == 0 and seq % CHUNK == 0
    tm_in = min(seq, 512)
    tm_out = min(seq, 256)
    tt_scan = min(seq, 64)
    tt_peer = 128
    n = bsz * seq
    nch = d // LANES

    l = 0
    mod3 = _ada(c, w_ada[l], b_ada[l]).reshape(bsz, N_MOD, d)
    ygm, zs = _mix_in(x, mod3, g_mix[l], w_in[l], sgu_ln_g[l], sgu_ln_b[l], w_s[l], b_s[l], tm_in)
    abr, abi, bbr, bbi = _s5_params(ssm_a_re[l], ssm_a_im[l], ssm_log_dt[l], ssm_b_re[l], ssm_b_im[l])
    yss = _s5(zs, abr[:, 0, :], abi[:, 0, :], bbr, bbi, ssm_c_re[l], ssm_c_im[l],
              ssm_d[l], w_glu[l], b_glu[l], tt_scan)
    x1, h2r, idx, gate = _mix_out(x, ygm, yss, mod3, w_out[l], g_ffn[l], w_q[l], peer_keys[l], tm_out)
    w = _peer_u(idx, h2r, gate, _pack_table(peer_u[l]), tt_peer)
    peer = _peer_v(idx, w, _pack_table(peer_v[l]), nch, tt_peer)
    return _final(x1, peer.reshape(n * nch, LANES), mod3, g_final, bsz, seq, tm_in)
```

```python
import functools
import math

import jax
import jax.numpy as jnp
from jax import lax
from jax.experimental import pallas as pl
from jax.experimental.pallas import tpu as pltpu

f32 = jnp.float32
bf16 = jnp.bfloat16
i32 = jnp.int32

EPS = 1e-6
LANES = 128
SUBLANES = 8
GM_HEADS = 4
CHUNK = 128
SSM_GROUPS = 32
SSM_GROUP = 16
SSM_STATE = 64
PEER_HEADS = 8
PEER_NKEYS = 128
PEER_TOPK = 16
PEER_DHALF = 64
N_MOD = 6
VMEM_LIMIT = 48 * 1024 * 1024
PEER_UNROLL = 8

_HIGHEST = lax.Precision.HIGHEST
_NT = (((1,), (1,)), ((), ()))


def _gelu(x):
    return 0.5 * x * (1.0 + jnp.tanh(0.7978845608028654 * (x + 0.044715 * (x * x * x))))


def _rms(x, g):
    return x * lax.rsqrt(jnp.mean(x * x, axis=-1, keepdims=True) + EPS) * g


def _ada_kernel(c_ref, w_ref, b_ref, o_ref):
    c = c_ref[...]
    cond = c * jax.nn.sigmoid(c)
    o_ref[...] = jnp.dot(cond, w_ref[...], precision=_HIGHEST, preferred_element_type=f32) + b_ref[...]


def _ada(c, w, b):
    bsz, d = c.shape
    n = w.shape[1]
    tn = 1536
    return pl.pallas_call(
        _ada_kernel,
        grid=(n // tn,),
        in_specs=[pl.BlockSpec((bsz, d), lambda j: (0, 0)),
                  pl.BlockSpec((d, tn), lambda j: (0, j)),
                  pl.BlockSpec((1, tn), lambda j: (0, j))],
        out_specs=pl.BlockSpec((bsz, tn), lambda j: (0, j)),
        out_shape=jax.ShapeDtypeStruct((bsz, n), f32),
        compiler_params=pltpu.CompilerParams(vmem_limit_bytes=VMEM_LIMIT),
        name="ada",
    )(c, w, b.reshape(1, n))


def _s5_params_kernel(are_ref, aim_ref, ldt_ref, bre_ref, bim_ref,
                      abr_ref, abi_ref, bbr_ref, bbi_ref):
    lre = jnp.minimum(are_ref[...], -1e-4)
    lim = aim_ref[...]
    dt = jnp.exp(ldt_ref[...])
    mag = jnp.exp(lre * dt)
    abr = mag * jnp.cos(lim * dt)
    abi = mag * jnp.sin(lim * dt)
    abr_ref[...] = abr
    abi_ref[...] = abi
    nr = abr - 1.0
    ni = abi
    den = lre * lre + lim * lim
    fr = (nr * lre + ni * lim) / den
    fi = (ni * lre - nr * lim) / den
    bre = bre_ref[...]
    bim = bim_ref[...]
    bbr_ref[...] = fr * bre - fi * bim
    bbi_ref[...] = fr * bim + fi * bre


def _s5_params(a_re, a_im, log_dt, b_re, b_im):
    g, p = a_re.shape
    hg = b_re.shape[2]
    are = a_re.reshape(g, 1, p)
    aim = a_im.reshape(g, 1, p)
    ldt = jnp.broadcast_to(log_dt.reshape(g, 1, 1), (g, 1, p))
    bre = jnp.transpose(b_re, (0, 2, 1))
    bim = jnp.transpose(b_im, (0, 2, 1))
    small = jax.ShapeDtypeStruct((g, 1, p), f32)
    big = jax.ShapeDtypeStruct((g, hg, p), f32)
    return pl.pallas_call(
        _s5_params_kernel,
        out_shape=(small, small, big, big),
        name="s5_params",
    )(are, aim, ldt, bre, bim)


def _mix_in_kernel(x_ref, mod_ref, g_ref, win_ref, lng_ref, lnb_ref, ws_ref, bs_ref,
                   ygm_ref, zs_ref):
    tm = x_ref.shape[0]
    gw = ygm_ref.shape[1]
    x = x_ref[...]
    h = _rms(x, g_ref[...]) * (1.0 + mod_ref[1:2, :]) + mod_ref[0:1, :]
    z = jnp.dot(h.astype(bf16), win_ref[...], preferred_element_type=f32)
    zs_ref[...] = z[:, 2 * gw:]
    u = _gelu(z[:, :gw])
    gv = _gelu(z[:, gw:2 * gw])
    mu = jnp.mean(gv, axis=-1, keepdims=True)
    dv = gv - mu
    var = jnp.mean(dv * dv, axis=-1, keepdims=True)
    v = (dv * lax.rsqrt(var + EPS) * lng_ref[...] + lnb_ref[...]).astype(bf16)
    row = lax.broadcasted_iota(i32, (CHUNK, CHUNK), 0)
    col = lax.broadcasted_iota(i32, (CHUNK, CHUNK), 1)
    causal = row >= col
    hd_w = gw // GM_HEADS
    for hd in range(GM_HEADS):
        wm = jnp.where(causal, ws_ref[hd], 0.0).astype(bf16)
        bias = bs_ref[hd]
        for ck in range(tm // CHUNK):
            rs = slice(ck * CHUNK, (ck + 1) * CHUNK)
            cs = slice(hd * hd_w, (hd + 1) * hd_w)
            mixed = jnp.dot(wm, v[rs, cs], preferred_element_type=f32) + bias
            ygm_ref[rs, cs] = (u[rs, cs] * mixed).astype(bf16)


def _mix_in(x, mod3, g_mix, w_in, ln_g, ln_b, w_s, b_s, tm):
    bsz, seq, d = x.shape
    gw = ln_g.shape[0]
    nin = w_in.shape[1]
    bs_b = jnp.broadcast_to(b_s[:, :, None], (GM_HEADS, CHUNK, gw // GM_HEADS))
    return pl.pallas_call(
        _mix_in_kernel,
        grid=(bsz, seq // tm),
        in_specs=[pl.BlockSpec((None, tm, d), lambda b, i: (b, i, 0)),
                  pl.BlockSpec((None, N_MOD, d), lambda b, i: (b, 0, 0)),
                  pl.BlockSpec((1, d), lambda b, i: (0, 0)),
                  pl.BlockSpec((d, nin), lambda b, i: (0, 0)),
                  pl.BlockSpec((1, gw), lambda b, i: (0, 0)),
                  pl.BlockSpec((1, gw), lambda b, i: (0, 0)),
                  pl.BlockSpec((GM_HEADS, CHUNK, CHUNK), lambda b, i: (0, 0, 0)),
                  pl.BlockSpec((GM_HEADS, CHUNK, gw // GM_HEADS), lambda b, i: (0, 0, 0))],
        out_specs=[pl.BlockSpec((None, tm, gw), lambda b, i: (b, i, 0)),
                   pl.BlockSpec((None, tm, nin - 2 * gw), lambda b, i: (b, i, 0))],
        out_shape=(jax.ShapeDtypeStruct((bsz, seq, gw), bf16),
                   jax.ShapeDtypeStruct((bsz, seq, nin - 2 * gw), f32)),
        compiler_params=pltpu.CompilerParams(
            dimension_semantics=("parallel", "parallel"), vmem_limit_bytes=VMEM_LIMIT),
        name="mix_in",
    )(x, mod3, g_mix.reshape(1, d), w_in.astype(bf16), ln_g.reshape(1, gw), ln_b.reshape(1, gw), w_s, bs_b)


def _s5_kernel(zs_ref, wb_ref, av_ref, wc_ref, dsk_ref, wglu_ref, bglu_ref, o_ref,
               zi, st, carry, yo):
    bsz, tt, w = zs_ref.shape
    nlt = w // LANES
    hw = st.shape[2] // 2
    nhalf = st.shape[0]

    @pl.when(pl.program_id(0) == 0)
    def _():
        carry[...] = jnp.zeros_like(carry)

    for b in range(bsz):
        zb = zs_ref[b]
        for j in range(nlt):
            zi[j, pl.ds(b, tt, stride=bsz), :] = zb[:, j * LANES:(j + 1) * LANES]

    ys = []
    lt_per_half = nlt // nhalf
    for hf in range(nhalf):
        zh = jnp.concatenate([zi[hf * lt_per_half + j] for j in range(lt_per_half)], axis=1)
        st[hf] = jnp.dot(zh.astype(bf16), wb_ref[hf], preferred_element_type=f32)
        ar = jnp.broadcast_to(av_ref[2 * hf:2 * hf + 1, :], (bsz, hw))
        ai = jnp.broadcast_to(av_ref[2 * hf + 1:2 * hf + 2, :], (bsz, hw))

        def step(t, xr, xi):
            r0 = pl.multiple_of(t * bsz, bsz)
            bur = st[hf, pl.ds(r0, bsz), 0:hw]
            bui = st[hf, pl.ds(r0, bsz), hw:2 * hw]
            nxr = ar * xr - ai * xi + bur
            nxi = ar * xi + ai * xr + bui
            st[hf, pl.ds(r0, bsz), 0:hw] = nxr
            st[hf, pl.ds(r0, bsz), hw:2 * hw] = nxi
            return nxr, nxi

        def body(t2, c):
            xr, xi = c
            xr, xi = step(2 * t2, xr, xi)
            xr, xi = step(2 * t2 + 1, xr, xi)
            return xr, xi

        xr, xi = lax.fori_loop(0, tt // 2, body, (carry[hf, :, 0:hw], carry[hf, :, hw:2 * hw]))
        carry[hf, :, 0:hw] = xr
        carry[hf, :, hw:2 * hw] = xi
        ys.append(jnp.dot(st[hf].astype(bf16), wc_ref[hf], preferred_element_type=f32))

    u = jnp.concatenate([zi[j] for j in range(nlt)], axis=1)
    y = jnp.concatenate(ys, axis=1) + dsk_ref[...] * u
    y = _gelu(y)
    gate = jax.nn.sigmoid(jnp.dot(y.astype(bf16), wglu_ref[...], preferred_element_type=f32) + bglu_ref[...])
    y = y * gate
    for j in range(nlt):
        yo[j] = y[:, j * LANES:(j + 1) * LANES]
    for b in range(bsz):
        o_ref[b] = jnp.concatenate(
            [yo[j, pl.ds(b, tt, stride=bsz), :] for j in range(nlt)], axis=1).astype(bf16)


def _s5(zs, abr, abi, bbr, bbi, c_re, c_im, d_skip, w_glu, b_glu, tt):
    bsz, seq, w = zs.shape
    g, hg, p = bbr.shape
    nhalf = 2
    gh = g // nhalf
    eye = jnp.eye(gh, dtype=f32)

    def blockdiag_in(m):
        return jnp.einsum('ghp,gk->ghkp', m, eye).reshape(gh * hg, gh * p)

    def blockdiag_out(m):
        return jnp.einsum('ghp,gk->gpkh', m, eye).reshape(gh * p, gh * hg)

    wb = jnp.stack([jnp.concatenate([blockdiag_in(bbr[h * gh:(h + 1) * gh]),
                                     blockdiag_in(bbi[h * gh:(h + 1) * gh])], axis=1)
                    for h in range(nhalf)]).astype(bf16)
    wc = jnp.stack([jnp.concatenate([blockdiag_out(c_re[h * gh:(h + 1) * gh]),
                                     blockdiag_out(-c_im[h * gh:(h + 1) * gh])], axis=0)
                    for h in range(nhalf)]).astype(bf16)
    av = jnp.concatenate([jnp.stack([abr[h * gh:(h + 1) * gh].reshape(gh * p),
                                     abi[h * gh:(h + 1) * gh].reshape(gh * p)])
                          for h in range(nhalf)], axis=0)
    hw = gh * p
    nlt = w // LANES
    return pl.pallas_call(
        _s5_kernel,
        grid=(seq // tt,),
        in_specs=[pl.BlockSpec((bsz, tt, w), lambda i: (0, i, 0)),
                  pl.BlockSpec(wb.shape, lambda i: (0, 0, 0)),
                  pl.BlockSpec(av.shape, lambda i: (0, 0)),
                  pl.BlockSpec(wc.shape, lambda i: (0, 0, 0)),
                  pl.BlockSpec((1, w), lambda i: (0, 0)),
                  pl.BlockSpec((w, w), lambda i: (0, 0)),
                  pl.BlockSpec((1, w), lambda i: (0, 0))],
        out_specs=pl.BlockSpec((bsz, tt, w), lambda i: (0, i, 0)),
        out_shape=jax.ShapeDtypeStruct((bsz, seq, w), bf16),
        scratch_shapes=[pltpu.VMEM((nlt, tt * bsz, LANES), f32),
                        pltpu.VMEM((nhalf, tt * bsz, 2 * hw), f32),
                        pltpu.VMEM((nhalf, bsz, 2 * hw), f32),
                        pltpu.VMEM((nlt, tt * bsz, LANES), f32)],
        compiler_params=pltpu.CompilerParams(
            dimension_semantics=("arbitrary",), vmem_limit_bytes=VMEM_LIMIT),
        name="s5",
    )(zs, wb, av, wc, d_skip.reshape(1, w), w_glu.astype(bf16), b_glu.reshape(1, w))


def _top16(vals, rowid):
    nrow = vals.shape[0]
    tops, topi = [], []
    for _ in range(PEER_TOPK):
        m = jnp.max(vals, axis=0, keepdims=True)
        am = jnp.min(jnp.where(vals == m, rowid, float(nrow)), axis=0, keepdims=True)
        tops.append(m)
        topi.append(am)
        vals = jnp.where(rowid == am, -jnp.inf, vals)
    return tops, topi


def _mix_out_kernel(x_ref, ygm_ref, yss_ref, mod_ref, wo1_ref, wo2_ref, g_ref, wqt_ref, keys_ref,
                    x1_ref, h2_ref, idx_ref, gate_ref):
    tm, d = x_ref.shape
    y = jnp.dot(ygm_ref[...], wo1_ref[...], preferred_element_type=f32)
    y = y + jnp.dot(yss_ref[...], wo2_ref[...], preferred_element_type=f32)
    x1 = x_ref[...] + mod_ref[2:3, :] * y
    x1_ref[...] = x1
    h2 = _rms(x1, g_ref[...]) * (1.0 + mod_ref[4:5, :]) + mod_ref[3:4, :]
    for c in range(d // LANES):
        h2_ref[pl.ds(c, tm, stride=d // LANES), :] = h2[:, c * LANES:(c + 1) * LANES]
    qt = lax.dot_general(wqt_ref[...], h2.astype(bf16), _NT, preferred_element_type=f32).astype(bf16)

    keyid = lax.broadcasted_iota(i32, (PEER_NKEYS, tm), 0).astype(f32)
    pos_rows = [float(j) for j in range(16)]
    for i in range(1, 8):
        pos_rows += [float(i * 16 + j) for j in range(8)]
    pos_rows += [float(i * 16) for i in range(8, 16)]
    ncand = len(pos_rows)
    prow = lax.broadcasted_iota(i32, (ncand, tm), 0)
    pos = jnp.where(prow < 16, prow,
                    jnp.where(prow < 72, jnp.right_shift(prow - 8, 3) * 16 + jnp.bitwise_and(prow, 7),
                              (prow - 64) * 16)).astype(f32)

    idx_rows, gate_rows = [], []
    for hd in range(PEER_HEADS):
        halves = []
        for c in range(2):
            off = hd * 2 * PEER_DHALF + c * PEER_DHALF
            s = jnp.dot(keys_ref[hd, c], qt[off:off + PEER_DHALF, :], preferred_element_type=f32)
            halves.append(_top16(s, keyid))
        (ta, ia), (tb, ib) = halves
        b16 = jnp.concatenate(tb, axis=0)
        ib16 = jnp.concatenate(ib, axis=0)
        b8, ib8 = b16[:8], ib16[:8]
        cand = [ta[0] + b16]
        cid = [ia[0] * float(PEER_NKEYS) + ib16]
        for i in range(1, 8):
            cand.append(ta[i] + b8)
            cid.append(ia[i] * float(PEER_NKEYS) + ib8)
        a_hi = jnp.concatenate(ta[8:], axis=0)
        ia_hi = jnp.concatenate(ia[8:], axis=0)
        cand.append(a_hi + tb[0])
        cid.append(ia_hi * float(PEER_NKEYS) + ib[0])
        cand = jnp.concatenate(cand, axis=0)
        nexp = float(PEER_NKEYS * PEER_NKEYS)
        key = pos * nexp + jnp.concatenate(cid, axis=0)
        best, eid = [], []
        for _ in range(PEER_TOPK):
            m = jnp.max(cand, axis=0, keepdims=True)
            kmin = jnp.min(jnp.where(cand == m, key, 1e9), axis=0, keepdims=True)
            eid.append(kmin - jnp.floor(kmin * (1.0 / nexp)) * nexp)
            best.append(m)
            cand = jnp.where(key == kmin, -jnp.inf, cand)
        best = jnp.concatenate(best, axis=0)
        e = jnp.exp(best - best[0:1])
        gate_rows.append(e / jnp.sum(e, axis=0, keepdims=True))
        idx_rows.append(jnp.concatenate(eid, axis=0))
    gate_t = jnp.concatenate(gate_rows, axis=0)
    idx_t = jnp.concatenate(idx_rows, axis=0)
    gate_ref[...] = gate_t.T
    idx_ref[...] = (idx_t.T * 4.0).astype(i32)


def _mix_out(x, ygm, yss, mod3, w_out, g_ffn, w_q, keys, tm):
    bsz, seq, d = x.shape
    gw = ygm.shape[2]
    nt = seq // tm
    nk = PEER_HEADS * PEER_TOPK
    n = bsz * seq
    wo = w_out.astype(bf16)
    return pl.pallas_call(
        _mix_out_kernel,
        grid=(bsz, nt),
        in_specs=[pl.BlockSpec((None, tm, d), lambda b, i: (b, i, 0)),
                  pl.BlockSpec((None, tm, gw), lambda b, i: (b, i, 0)),
                  pl.BlockSpec((None, tm, gw), lambda b, i: (b, i, 0)),
                  pl.BlockSpec((None, N_MOD, d), lambda b, i: (b, 0, 0)),
                  pl.BlockSpec((gw, d), lambda b, i: (0, 0)),
                  pl.BlockSpec((gw, d), lambda b, i: (0, 0)),
                  pl.BlockSpec((1, d), lambda b, i: (0, 0)),
                  pl.BlockSpec((d, d), lambda b, i: (0, 0)),
                  pl.BlockSpec(keys.shape, lambda b, i: (0, 0, 0, 0))],
        out_specs=[pl.BlockSpec((tm, d), lambda b, i: (b * nt + i, 0)),
                   pl.BlockSpec((tm * (d // LANES), LANES), lambda b, i: (b * nt + i, 0)),
                   pl.BlockSpec((tm, nk), lambda b, i: (b * nt + i, 0)),
                   pl.BlockSpec((tm, nk), lambda b, i: (b * nt + i, 0))],
        out_shape=(jax.ShapeDtypeStruct((n, d), f32),
                   jax.ShapeDtypeStruct((n * (d // LANES), LANES), f32),
                   jax.ShapeDtypeStruct((n, nk), i32),
                   jax.ShapeDtypeStruct((n, nk), f32)),
        compiler_params=pltpu.CompilerParams(
            dimension_semantics=("parallel", "parallel"), vmem_limit_bytes=VMEM_LIMIT),
        name="mix_out",
    )(x, ygm, yss, mod3, wo[:gw], wo[gw:], g_ffn.reshape(1, d), w_q.T.astype(bf16), keys.astype(bf16))


def _pack_table_kernel(t_ref, o_ref):
    eb, d = t_ref.shape
    rows = d // (2 * LANES)
    x = t_ref[...].astype(bf16).astype(f32)
    bits = lax.bitcast_convert_type(x, i32)
    for s in range(rows):
        lo = lax.shift_right_logical(bits[:, 2 * s * LANES:(2 * s + 1) * LANES], 16)
        hi = jnp.bitwise_and(bits[:, (2 * s + 1) * LANES:(2 * s + 2) * LANES], -65536)
        o_ref[pl.ds(s, eb, stride=rows), :] = jnp.bitwise_or(lo, hi)


def _pack_table(tab):
    e, d = tab.shape
    rows = d // (2 * LANES)
    eb = 512
    return pl.pallas_call(
        _pack_table_kernel,
        grid=(e // eb,),
        in_specs=[pl.BlockSpec((eb, d), lambda i: (i, 0))],
        out_specs=pl.BlockSpec((eb * rows, LANES), lambda i: (i, 0)),
        out_shape=jax.ShapeDtypeStruct((e * rows, LANES), i32),
        compiler_params=pltpu.CompilerParams(
            dimension_semantics=("parallel",), vmem_limit_bytes=VMEM_LIMIT),
        name="pack_table",
    )(tab)


def _gather_rows(idx_ref, ts, tab_ref, stages, nk, rows):
    row_refs = [idx_ref.at[pl.ds(t, 1)] for t in ts]
    for k in range(nk):
        for row_ref, stage in zip(row_refs, stages):
            r = pl.multiple_of(row_ref[0, k], rows)
            stage[rows * k:rows * (k + 1), :] = tab_ref[pl.ds(r, rows), :]
    return [pltpu.bitcast(stage[...], bf16) for stage in stages]


def _split_bf16(x):
    hi = x.astype(bf16)
    return hi, (x - hi.astype(f32)).astype(bf16)


def _peer_u_kernel(idx_ref, h_ref, gate_ref, tab_ref, w_ref, *scratch):
    stages, res_all = scratch[:-1], scratch[-1]
    tt, nk = gate_ref.shape
    nch = h_ref.shape[1]
    rows = nch // 2
    d = nch * LANES
    sh = int(math.log2(nch))
    nun = len(stages)

    def body(i, carry):
        ts = [i * nun + u for u in range(nun)]
        ms = _gather_rows(idx_ref, ts, tab_ref, stages, nk, rows)
        for t, m in zip(ts, ms):
            hb = h_ref[t].astype(bf16)
            r0 = pl.multiple_of(t * nch, nch)
            res_all[pl.ds(r0, nch), :] = lax.dot_general(hb, m, _NT, preferred_element_type=f32)
        return carry

    lax.fori_loop(0, tt // nun, body, 0)

    c_i = lax.broadcasted_iota(i32, (nch, d), 0)
    l_i = lax.broadcasted_iota(i32, (nch, d), 1)
    diag = (jnp.bitwise_and(l_i, nch - 1) == c_i).astype(f32)
    masked = (res_all[...].reshape(tt, nch, d) * diag).reshape(tt * nch, d)
    grp = (jnp.right_shift(lax.broadcasted_iota(i32, (d, nk), 0), sh)
           == lax.broadcasted_iota(i32, (d, nk), 1)).astype(f32).astype(bf16)
    mhi, mlo = _split_bf16(masked)
    g = jnp.dot(mhi, grp, preferred_element_type=f32) + jnp.dot(mlo, grp, preferred_element_type=f32)
    sel = (jnp.right_shift(lax.broadcasted_iota(i32, (tt, tt * nch), 1), sh)
           == lax.broadcasted_iota(i32, (tt, tt * nch), 0)).astype(f32).astype(bf16)
    ghi, glo = _split_bf16(g)
    act = jnp.dot(sel, ghi, preferred_element_type=f32) + jnp.dot(sel, glo, preferred_element_type=f32)
    w_ref[...] = gate_ref[...] * _gelu(act)


def _peer_v_kernel(idx_ref, w_ref, tab_ref, o_ref, *scratch):
    stages, whi, wlo = scratch[:-2], scratch[-2], scratch[-1]
    tt, nk = w_ref.shape
    nch = o_ref.shape[1]
    rows = nch // 2
    d = nch * LANES
    sh = int(math.log2(nch))
    nun = len(stages)
    k_i = lax.broadcasted_iota(i32, (nk, d), 0)
    n_i = lax.broadcasted_iota(i32, (nk, d), 1)
    expand = (jnp.right_shift(n_i, sh) == k_i).astype(f32).astype(bf16)
    hi, lo = _split_bf16(w_ref[...])
    whi[...] = jnp.dot(hi, expand, preferred_element_type=f32)
    wlo[...] = jnp.dot(lo, expand, preferred_element_type=f32)
    c_i = lax.broadcasted_iota(i32, (nch, d), 0)
    l_i = lax.broadcasted_iota(i32, (nch, d), 1)
    diag = (jnp.bitwise_and(l_i, nch - 1) == c_i).astype(f32)

    def body(i, carry):
        ts = [i * nun + u for u in range(nun)]
        ms = _gather_rows(idx_ref, ts, tab_ref, stages, nk, rows)
        for t, m in zip(ts, ms):
            lhs = jnp.concatenate([whi[pl.ds(t, 1), :] * diag, wlo[pl.ds(t, 1), :] * diag], axis=0).astype(bf16)
            out = jnp.dot(lhs, m, preferred_element_type=f32)
            o_ref[t] = out[:nch] + out[nch:]
        return carry

    lax.fori_loop(0, tt // nun, body, 0)


def _peer_u(idx, h2r, gate, tab, tt):
    n, nk = gate.shape
    nch = h2r.shape[0] // n
    h3 = h2r.reshape(n, nch, LANES)
    return pl.pallas_call(
        _peer_u_kernel,
        grid=(n // tt,),
        in_specs=[pl.BlockSpec((tt, nk), lambda i: (i, 0), memory_space=pltpu.SMEM),
                  pl.BlockSpec((tt, nch, LANES), lambda i: (i, 0, 0)),
                  pl.BlockSpec((tt, nk), lambda i: (i, 0)),
                  pl.BlockSpec(tab.shape, lambda i: (0, 0), pipeline_mode=pl.Buffered(1))],
        out_specs=pl.BlockSpec((tt, nk), lambda i: (i, 0)),
        out_shape=jax.ShapeDtypeStruct((n, nk), f32),
        scratch_shapes=[pltpu.VMEM((nk * nch // 2, LANES), i32)] * PEER_UNROLL
        + [pltpu.VMEM((tt * nch, nch * LANES), f32)],
        compiler_params=pltpu.CompilerParams(
            dimension_semantics=("parallel",), vmem_limit_bytes=VMEM_LIMIT),
        name="peer_u",
    )(idx, h3, gate, tab)


def _peer_v(idx, w, tab, nch, tt):
    n, nk = w.shape
    d = nch * LANES
    return pl.pallas_call(
        _peer_v_kernel,
        grid=(n // tt,),
        in_specs=[pl.BlockSpec((tt, nk), lambda i: (i, 0), memory_space=pltpu.SMEM),
                  pl.BlockSpec((tt, nk), lambda i: (i, 0)),
                  pl.BlockSpec(tab.shape, lambda i: (0, 0), pipeline_mode=pl.Buffered(1))],
        out_specs=pl.BlockSpec((tt, nch, LANES), lambda i: (i, 0, 0)),
        out_shape=jax.ShapeDtypeStruct((n, nch, LANES), f32),
        scratch_shapes=[pltpu.VMEM((nk * nch // 2, LANES), i32)] * (2 * PEER_UNROLL)
        + [pltpu.VMEM((tt, d), f32), pltpu.VMEM((tt, d), f32)],
        compiler_params=pltpu.CompilerParams(
            dimension_semantics=("parallel",), vmem_limit_bytes=VMEM_LIMIT),
        name="peer_v",
    )(idx, w, tab)


def _final_kernel(x1_ref, p_ref, mod_ref, g_ref, o_ref):
    tm, d = x1_ref.shape
    nch = d // LANES
    peer = jnp.concatenate([p_ref[pl.ds(c, tm, stride=nch), :] for c in range(nch)], axis=1)
    x2 = x1_ref[...] + mod_ref[5:6, :] * peer
    o_ref[...] = _rms(x2, g_ref[...])


def _final(x1, peer2d, mod3, g_final, bsz, seq, tm):
    n, d = x1.shape
    nt = seq // tm
    nch = d // LANES
    return pl.pallas_call(
        _final_kernel,
        grid=(bsz, nt),
        in_specs=[pl.BlockSpec((tm, d), lambda b, i: (b * nt + i, 0)),
                  pl.BlockSpec((tm * nch, LANES), lambda b, i: (b * nt + i, 0)),
                  pl.BlockSpec((None, N_MOD, d), lambda b, i: (b, 0, 0)),
                  pl.BlockSpec((1, d), lambda b, i: (0, 0))],
        out_specs=pl.BlockSpec((None, tm, d), lambda b, i: (b, i, 0)),
        out_shape=jax.ShapeDtypeStruct((bsz, seq, d), f32),
        compiler_params=pltpu.CompilerParams(
            dimension_semantics=("parallel", "parallel"), vmem_limit_bytes=VMEM_LIMIT),
        name="final",
    )(x1, peer2d, mod3, g_final.reshape(1, d))


def kernel(x, c, w_ada, b_ada, g_mix, w_in, sgu_ln_g, sgu_ln_b, w_s, b_s, ssm_a_re, ssm_a_im, ssm_log_dt, ssm_b_re, ssm_b_im, ssm_c_re, ssm_c_im, ssm_d, w_glu, b_glu, w_out, g_ffn, w_q, peer_keys, peer_u, peer_v, g_final):
    bsz, seq, d = x.shape
    depth = w_ada.shape[0]
    assert depth == 1 and d % LANES == 0 and seq % CHUNK == 0
    tm_in = min(seq, 512)
    tm_out = min(seq, 256)
    tt_scan = min(seq, 64)
    tt_peer = 128
    n = bsz * seq
    nch = d // LANES

    l = 0
    mod3 = _ada(c, w_ada[l], b_ada[l]).reshape(bsz, N_MOD, d)
    ygm, zs = _mix_in(x, mod3, g_mix[l], w_in[l], sgu_ln_g[l], sgu_ln_b[l], w_s[l], b_s[l], tm_in)
    abr, abi, bbr, bbi = _s5_params(ssm_a_re[l], ssm_a_im[l], ssm_log_dt[l], ssm_b_re[l], ssm_b_im[l])
    yss = _s5(zs, abr[:, 0, :], abi[:, 0, :], bbr, bbi, ssm_c_re[l], ssm_c_im[l],
              ssm_d[l], w_glu[l], b_glu[l], tt_scan)
    x1, h2r, idx, gate = _mix_out(x, ygm, yss, mod3, w_out[l], g_ffn[l], w_q[l], peer_keys[l], tm_out)
    w = _peer_u(idx, h2r, gate, _pack_table(peer_u[l]), tt_peer)
    peer = _peer_v(idx, w, _pack_table(peer_v[l]), nch, tt_peer)
    return _final(x1, peer.reshape(n * nch, LANES), mod3, g_final, bsz, seq, tm_in)
```

```python
import functools
import math

import jax
import jax.numpy as jnp
from jax import lax
from jax.experimental import pallas as pl
from jax.experimental.pallas import tpu as pltpu

f32 = jnp.float32
bf16 = jnp.bfloat16
i32 = jnp.int32

EPS = 1e-6
LANES = 128
SUBLANES = 8
GM_HEADS = 4
CHUNK = 128
SSM_GROUPS = 32
SSM_GROUP = 16
SSM_STATE = 64
PEER_HEADS = 8
PEER_NKEYS = 128
PEER_TOPK = 16
PEER_DHALF = 64
N_MOD = 6
VMEM_LIMIT = 48 * 1024 * 1024
PEER_UNROLL = 16

_HIGHEST = lax.Precision.HIGHEST
_NT = (((1,), (1,)), ((), ()))


def _gelu(x):
    return 0.5 * x * (1.0 + jnp.tanh(0.7978845608028654 * (x + 0.044715 * (x * x * x))))


def _rms(x, g):
    return x * lax.rsqrt(jnp.mean(x * x, axis=-1, keepdims=True) + EPS) * g


def _ada_kernel(c_ref, w_ref, b_ref, o_ref):
    c = c_ref[...]
    cond = c * jax.nn.sigmoid(c)
    o_ref[...] = jnp.dot(cond, w_ref[...], precision=_HIGHEST, preferred_element_type=f32) + b_ref[...]


def _ada(c, w, b):
    bsz, d = c.shape
    n = w.shape[1]
    tn = 1536
    return pl.pallas_call(
        _ada_kernel,
        grid=(n // tn,),
        in_specs=[pl.BlockSpec((bsz, d), lambda j: (0, 0)),
                  pl.BlockSpec((d, tn), lambda j: (0, j)),
                  pl.BlockSpec((1, tn), lambda j: (0, j))],
        out_specs=pl.BlockSpec((bsz, tn), lambda j: (0, j)),
        out_shape=jax.ShapeDtypeStruct((bsz, n), f32),
        compiler_params=pltpu.CompilerParams(vmem_limit_bytes=VMEM_LIMIT),
        name="ada",
    )(c, w, b.reshape(1, n))


def _s5_params_kernel(are_ref, aim_ref, ldt_ref, bre_ref, bim_ref,
                      abr_ref, abi_ref, bbr_ref, bbi_ref):
    lre = jnp.minimum(are_ref[...], -1e-4)
    lim = aim_ref[...]
    dt = jnp.exp(ldt_ref[...])
    mag = jnp.exp(lre * dt)
    abr = mag * jnp.cos(lim * dt)
    abi = mag * jnp.sin(lim * dt)
    abr_ref[...] = abr
    abi_ref[...] = abi
    nr = abr - 1.0
    ni = abi
    den = lre * lre + lim * lim
    fr = (nr * lre + ni * lim) / den
    fi = (ni * lre - nr * lim) / den
    bre = bre_ref[...]
    bim = bim_ref[...]
    bbr_ref[...] = fr * bre - fi * bim
    bbi_ref[...] = fr * bim + fi * bre


def _s5_params(a_re, a_im, log_dt, b_re, b_im):
    g, p = a_re.shape
    hg = b_re.shape[2]
    are = a_re.reshape(g, 1, p)
    aim = a_im.reshape(g, 1, p)
    ldt = jnp.broadcast_to(log_dt.reshape(g, 1, 1), (g, 1, p))
    bre = jnp.transpose(b_re, (0, 2, 1))
    bim = jnp.transpose(b_im, (0, 2, 1))
    small = jax.ShapeDtypeStruct((g, 1, p), f32)
    big = jax.ShapeDtypeStruct((g, hg, p), f32)
    return pl.pallas_call(
        _s5_params_kernel,
        out_shape=(small, small, big, big),
        name="s5_params",
    )(are, aim, ldt, bre, bim)


def _mix_in_kernel(x_ref, mod_ref, g_ref, win_ref, lng_ref, lnb_ref, ws_ref, bs_ref,
                   ygm_ref, zs_ref):
    tm = x_ref.shape[0]
    gw = ygm_ref.shape[1]
    x = x_ref[...]
    h = _rms(x, g_ref[...]) * (1.0 + mod_ref[1:2, :]) + mod_ref[0:1, :]
    z = jnp.dot(h.astype(bf16), win_ref[...], preferred_element_type=f32)
    zs_ref[...] = z[:, 2 * gw:]
    u = _gelu(z[:, :gw])
    gv = _gelu(z[:, gw:2 * gw])
    mu = jnp.mean(gv, axis=-1, keepdims=True)
    dv = gv - mu
    var = jnp.mean(dv * dv, axis=-1, keepdims=True)
    v = (dv * lax.rsqrt(var + EPS) * lng_ref[...] + lnb_ref[...]).astype(bf16)
    row = lax.broadcasted_iota(i32, (CHUNK, CHUNK), 0)
    col = lax.broadcasted_iota(i32, (CHUNK, CHUNK), 1)
    causal = row >= col
    hd_w = gw // GM_HEADS
    for hd in range(GM_HEADS):
        wm = jnp.where(causal, ws_ref[hd], 0.0).astype(bf16)
        bias = bs_ref[hd]
        for ck in range(tm // CHUNK):
            rs = slice(ck * CHUNK, (ck + 1) * CHUNK)
            cs = slice(hd * hd_w, (hd + 1) * hd_w)
            mixed = jnp.dot(wm, v[rs, cs], preferred_element_type=f32) + bias
            ygm_ref[rs, cs] = (u[rs, cs] * mixed).astype(bf16)


def _mix_in(x, mod3, g_mix, w_in, ln_g, ln_b, w_s, b_s, tm):
    bsz, seq, d = x.shape
    gw = ln_g.shape[0]
    nin = w_in.shape[1]
    bs_b = jnp.broadcast_to(b_s[:, :, None], (GM_HEADS, CHUNK, gw // GM_HEADS))
    return pl.pallas_call(
        _mix_in_kernel,
        grid=(bsz, seq // tm),
        in_specs=[pl.BlockSpec((None, tm, d), lambda b, i: (b, i, 0)),
                  pl.BlockSpec((None, N_MOD, d), lambda b, i: (b, 0, 0)),
                  pl.BlockSpec((1, d), lambda b, i: (0, 0)),
                  pl.BlockSpec((d, nin), lambda b, i: (0, 0)),
                  pl.BlockSpec((1, gw), lambda b, i: (0, 0)),
                  pl.BlockSpec((1, gw), lambda b, i: (0, 0)),
                  pl.BlockSpec((GM_HEADS, CHUNK, CHUNK), lambda b, i: (0, 0, 0)),
                  pl.BlockSpec((GM_HEADS, CHUNK, gw // GM_HEADS), lambda b, i: (0, 0, 0))],
        out_specs=[pl.BlockSpec((None, tm, gw), lambda b, i: (b, i, 0)),
                   pl.BlockSpec((None, tm, nin - 2 * gw), lambda b, i: (b, i, 0))],
        out_shape=(jax.ShapeDtypeStruct((bsz, seq, gw), bf16),
                   jax.ShapeDtypeStruct((bsz, seq, nin - 2 * gw), f32)),
        compiler_params=pltpu.CompilerParams(
            dimension_semantics=("parallel", "parallel"), vmem_limit_bytes=VMEM_LIMIT),
        name="mix_in",
    )(x, mod3, g_mix.reshape(1, d), w_in.astype(bf16), ln_g.reshape(1, gw), ln_b.reshape(1, gw), w_s, bs_b)


def _s5_kernel(zs_ref, wb_ref, av_ref, wc_ref, dsk_ref, wglu_ref, bglu_ref, o_ref,
               zi, st, carry, yo):
    bsz, tt, w = zs_ref.shape
    nlt = w // LANES
    hw = st.shape[2] // 2
    nhalf = st.shape[0]

    @pl.when(pl.program_id(0) == 0)
    def _():
        carry[...] = jnp.zeros_like(carry)

    for b in range(bsz):
        zb = zs_ref[b]
        for j in range(nlt):
            zi[j, pl.ds(b, tt, stride=bsz), :] = zb[:, j * LANES:(j + 1) * LANES]

    ys = []
    lt_per_half = nlt // nhalf
    for hf in range(nhalf):
        zh = jnp.concatenate([zi[hf * lt_per_half + j] for j in range(lt_per_half)], axis=1)
        st[hf] = jnp.dot(zh.astype(bf16), wb_ref[hf], preferred_element_type=f32)
        ar = jnp.broadcast_to(av_ref[2 * hf:2 * hf + 1, :], (bsz, hw))
        ai = jnp.broadcast_to(av_ref[2 * hf + 1:2 * hf + 2, :], (bsz, hw))

        def step(t, xr, xi):
            r0 = pl.multiple_of(t * bsz, bsz)
            bur = st[hf, pl.ds(r0, bsz), 0:hw]
            bui = st[hf, pl.ds(r0, bsz), hw:2 * hw]
            nxr = ar * xr - ai * xi + bur
            nxi = ar * xi + ai * xr + bui
            st[hf, pl.ds(r0, bsz), 0:hw] = nxr
            st[hf, pl.ds(r0, bsz), hw:2 * hw] = nxi
            return nxr, nxi

        def body(t2, c):
            xr, xi = c
            xr, xi = step(2 * t2, xr, xi)
            xr, xi = step(2 * t2 + 1, xr, xi)
            return xr, xi

        xr, xi = lax.fori_loop(0, tt // 2, body, (carry[hf, :, 0:hw], carry[hf, :, hw:2 * hw]))
        carry[hf, :, 0:hw] = xr
        carry[hf, :, hw:2 * hw] = xi
        ys.append(jnp.dot(st[hf].astype(bf16), wc_ref[hf], preferred_element_type=f32))

    u = jnp.concatenate([zi[j] for j in range(nlt)], axis=1)
    y = jnp.concatenate(ys, axis=1) + dsk_ref[...] * u
    y = _gelu(y)
    gate = jax.nn.sigmoid(jnp.dot(y.astype(bf16), wglu_ref[...], preferred_element_type=f32) + bglu_ref[...])
    y = y * gate
    for j in range(nlt):
        yo[j] = y[:, j * LANES:(j + 1) * LANES]
    for b in range(bsz):
        o_ref[b] = jnp.concatenate(
            [yo[j, pl.ds(b, tt, stride=bsz), :] for j in range(nlt)], axis=1).astype(bf16)


def _s5(zs, abr, abi, bbr, bbi, c_re, c_im, d_skip, w_glu, b_glu, tt):
    bsz, seq, w = zs.shape
    g, hg, p = bbr.shape
    nhalf = 2
    gh = g // nhalf
    eye = jnp.eye(gh, dtype=f32)

    def blockdiag_in(m):
        return jnp.einsum('ghp,gk->ghkp', m, eye).reshape(gh * hg, gh * p)

    def blockdiag_out(m):
        return jnp.einsum('ghp,gk->gpkh', m, eye).reshape(gh * p, gh * hg)

    wb = jnp.stack([jnp.concatenate([blockdiag_in(bbr[h * gh:(h + 1) * gh]),
                                     blockdiag_in(bbi[h * gh:(h + 1) * gh])], axis=1)
                    for h in range(nhalf)]).astype(bf16)
    wc = jnp.stack([jnp.concatenate([blockdiag_out(c_re[h * gh:(h + 1) * gh]),
                                     blockdiag_out(-c_im[h * gh:(h + 1) * gh])], axis=0)
                    for h in range(nhalf)]).astype(bf16)
    av = jnp.concatenate([jnp.stack([abr[h * gh:(h + 1) * gh].reshape(gh * p),
                                     abi[h * gh:(h + 1) * gh].reshape(gh * p)])
                          for h in range(nhalf)], axis=0)
    hw = gh * p
    nlt = w // LANES
    return pl.pallas_call(
        _s5_kernel,
        grid=(seq // tt,),
        in_specs=[pl.BlockSpec((bsz, tt, w), lambda i: (0, i, 0)),
                  pl.BlockSpec(wb.shape, lambda i: (0, 0, 0)),
                  pl.BlockSpec(av.shape, lambda i: (0, 0)),
                  pl.BlockSpec(wc.shape, lambda i: (0, 0, 0)),
                  pl.BlockSpec((1, w), lambda i: (0, 0)),
                  pl.BlockSpec((w, w), lambda i: (0, 0)),
                  pl.BlockSpec((1, w), lambda i: (0, 0))],
        out_specs=pl.BlockSpec((bsz, tt, w), lambda i: (0, i, 0)),
        out_shape=jax.ShapeDtypeStruct((bsz, seq, w), bf16),
        scratch_shapes=[pltpu.VMEM((nlt, tt * bsz, LANES), f32),
                        pltpu.VMEM((nhalf, tt * bsz, 2 * hw), f32),
                        pltpu.VMEM((nhalf, bsz, 2 * hw), f32),
                        pltpu.VMEM((nlt, tt * bsz, LANES), f32)],
        compiler_params=pltpu.CompilerParams(
            dimension_semantics=("arbitrary",), vmem_limit_bytes=VMEM_LIMIT),
        name="s5",
    )(zs, wb, av, wc, d_skip.reshape(1, w), w_glu.astype(bf16), b_glu.reshape(1, w))


def _top16(vals, rowid):
    nrow = vals.shape[0]
    tops, topi = [], []
    for _ in range(PEER_TOPK):
        m = jnp.max(vals, axis=0, keepdims=True)
        am = jnp.min(jnp.where(vals == m, rowid, float(nrow)), axis=0, keepdims=True)
        tops.append(m)
        topi.append(am)
        vals = jnp.where(rowid == am, -jnp.inf, vals)
    return tops, topi


def _mix_out_kernel(x_ref, ygm_ref, yss_ref, mod_ref, wo1_ref, wo2_ref, g_ref, wqt_ref, keys_ref,
                    x1_ref, h2_ref, idx_ref, gate_ref):
    tm, d = x_ref.shape
    y = jnp.dot(ygm_ref[...], wo1_ref[...], preferred_element_type=f32)
    y = y + jnp.dot(yss_ref[...], wo2_ref[...], preferred_element_type=f32)
    x1 = x_ref[...] + mod_ref[2:3, :] * y
    x1_ref[...] = x1
    h2 = _rms(x1, g_ref[...]) * (1.0 + mod_ref[4:5, :]) + mod_ref[3:4, :]
    for c in range(d // LANES):
        h2_ref[pl.ds(c, tm, stride=d // LANES), :] = h2[:, c * LANES:(c + 1) * LANES]
    qt = lax.dot_general(wqt_ref[...], h2.astype(bf16), _NT, preferred_element_type=f32).astype(bf16)

    keyid = lax.broadcasted_iota(i32, (PEER_NKEYS, tm), 0).astype(f32)
    pos_rows = [float(j) for j in range(16)]
    for i in range(1, 8):
        pos_rows += [float(i * 16 + j) for j in range(8)]
    pos_rows += [float(i * 16) for i in range(8, 16)]
    ncand = len(pos_rows)
    prow = lax.broadcasted_iota(i32, (ncand, tm), 0)
    pos = jnp.where(prow < 16, prow,
                    jnp.where(prow < 72, jnp.right_shift(prow - 8, 3) * 16 + jnp.bitwise_and(prow, 7),
                              (prow - 64) * 16)).astype(f32)

    idx_rows, gate_rows = [], []
    for hd in range(PEER_HEADS):
        halves = []
        for c in range(2):
            off = hd * 2 * PEER_DHALF + c * PEER_DHALF
            s = jnp.dot(keys_ref[hd, c], qt[off:off + PEER_DHALF, :], preferred_element_type=f32)
            halves.append(_top16(s, keyid))
        (ta, ia), (tb, ib) = halves
        b16 = jnp.concatenate(tb, axis=0)
        ib16 = jnp.concatenate(ib, axis=0)
        b8, ib8 = b16[:8], ib16[:8]
        cand = [ta[0] + b16]
        cid = [ia[0] * float(PEER_NKEYS) + ib16]
        for i in range(1, 8):
            cand.append(ta[i] + b8)
            cid.append(ia[i] * float(PEER_NKEYS) + ib8)
        a_hi = jnp.concatenate(ta[8:], axis=0)
        ia_hi = jnp.concatenate(ia[8:], axis=0)
        cand.append(a_hi + tb[0])
        cid.append(ia_hi * float(PEER_NKEYS) + ib[0])
        cand = jnp.concatenate(cand, axis=0)
        nexp = float(PEER_NKEYS * PEER_NKEYS)
        key = pos * nexp + jnp.concatenate(cid, axis=0)
        best, eid = [], []
        for _ in range(PEER_TOPK):
            m = jnp.max(cand, axis=0, keepdims=True)
            kmin = jnp.min(jnp.where(cand == m, key, 1e9), axis=0, keepdims=True)
            eid.append(kmin - jnp.floor(kmin * (1.0 / nexp)) * nexp)
            best.append(m)
            cand = jnp.where(key == kmin, -jnp.inf, cand)
        best = jnp.concatenate(best, axis=0)
        e = jnp.exp(best - best[0:1])
        gate_rows.append(e / jnp.sum(e, axis=0, keepdims=True))
        idx_rows.append(jnp.concatenate(eid, axis=0))
    gate_t = jnp.concatenate(gate_rows, axis=0)
    idx_t = jnp.concatenate(idx_rows, axis=0)
    gate_ref[...] = gate_t.T
    idx_ref[...] = (idx_t.T * 4.0).astype(i32)


def _mix_out(x, ygm, yss, mod3, w_out, g_ffn, w_q, keys, tm):
    bsz, seq, d = x.shape
    gw = ygm.shape[2]
    nt = seq // tm
    nk = PEER_HEADS * PEER_TOPK
    n = bsz * seq
    wo = w_out.astype(bf16)
    return pl.pallas_call(
        _mix_out_kernel,
        grid=(bsz, nt),
        in_specs=[pl.BlockSpec((None, tm, d), lambda b, i: (b, i, 0)),
                  pl.BlockSpec((None, tm, gw), lambda b, i: (b, i, 0)),
                  pl.BlockSpec((None, tm, gw), lambda b, i: (b, i, 0)),
                  pl.BlockSpec((None, N_MOD, d), lambda b, i: (b, 0, 0)),
                  pl.BlockSpec((gw, d), lambda b, i: (0, 0)),
                  pl.BlockSpec((gw, d), lambda b, i: (0, 0)),
                  pl.BlockSpec((1, d), lambda b, i: (0, 0)),
                  pl.BlockSpec((d, d), lambda b, i: (0, 0)),
                  pl.BlockSpec(keys.shape, lambda b, i: (0, 0, 0, 0))],
        out_specs=[pl.BlockSpec((tm, d), lambda b, i: (b * nt + i, 0)),
                   pl.BlockSpec((tm * (d // LANES), LANES), lambda b, i: (b * nt + i, 0)),
                   pl.BlockSpec((tm, nk), lambda b, i: (b * nt + i, 0)),
                   pl.BlockSpec((tm, nk), lambda b, i: (b * nt + i, 0))],
        out_shape=(jax.ShapeDtypeStruct((n, d), f32),
                   jax.ShapeDtypeStruct((n * (d // LANES), LANES), f32),
                   jax.ShapeDtypeStruct((n, nk), i32),
                   jax.ShapeDtypeStruct((n, nk), f32)),
        compiler_params=pltpu.CompilerParams(
            dimension_semantics=("parallel", "parallel"), vmem_limit_bytes=VMEM_LIMIT),
        name="mix_out",
    )(x, ygm, yss, mod3, wo[:gw], wo[gw:], g_ffn.reshape(1, d), w_q.T.astype(bf16), keys.astype(bf16))


def _pack_table_kernel(t_ref, o_ref):
    eb, d = t_ref.shape
    rows = d // (2 * LANES)
    x = t_ref[...].astype(bf16).astype(f32)
    bits = lax.bitcast_convert_type(x, i32)
    for s in range(rows):
        lo = lax.shift_right_logical(bits[:, 2 * s * LANES:(2 * s + 1) * LANES], 16)
        hi = jnp.bitwise_and(bits[:, (2 * s + 1) * LANES:(2 * s + 2) * LANES], -65536)
        o_ref[pl.ds(s, eb, stride=rows), :] = jnp.bitwise_or(lo, hi)


def _pack_table(tab):
    e, d = tab.shape
    rows = d // (2 * LANES)
    eb = 512
    return pl.pallas_call(
        _pack_table_kernel,
        grid=(e // eb,),
        in_specs=[pl.BlockSpec((eb, d), lambda i: (i, 0))],
        out_specs=pl.BlockSpec((eb * rows, LANES), lambda i: (i, 0)),
        out_shape=jax.ShapeDtypeStruct((e * rows, LANES), i32),
        compiler_params=pltpu.CompilerParams(
            dimension_semantics=("parallel",), vmem_limit_bytes=VMEM_LIMIT),
        name="pack_table",
    )(tab)


def _gather_rows(idx_ref, ts, tab_ref, nk, rows):
    row_refs = [idx_ref.at[pl.ds(t, 1)] for t in ts]
    out = []
    for row_ref in row_refs:
        slabs = [tab_ref[pl.ds(pl.multiple_of(row_ref[0, k], rows), rows), :] for k in range(nk)]
        out.append(pltpu.bitcast(jnp.concatenate(slabs, axis=0), bf16))
    return out


def _split_bf16(x):
    hi = x.astype(bf16)
    return hi, (x - hi.astype(f32)).astype(bf16)


def _peer_u_kernel(idx_ref, h_ref, gate_ref, tab_ref, w_ref, res_all):
    tt, nk = gate_ref.shape
    nch = h_ref.shape[1]
    rows = nch // 2
    d = nch * LANES
    sh = int(math.log2(nch))
    nun = PEER_UNROLL

    def body(i, carry):
        ts = [i * nun + u for u in range(nun)]
        ms = _gather_rows(idx_ref, ts, tab_ref, nk, rows)
        for t, m in zip(ts, ms):
            hb = h_ref[t].astype(bf16)
            r0 = pl.multiple_of(t * nch, nch)
            res_all[pl.ds(r0, nch), :] = lax.dot_general(hb, m, _NT, preferred_element_type=f32)
        return carry

    lax.fori_loop(0, tt // nun, body, 0)

    c_i = lax.broadcasted_iota(i32, (nch, d), 0)
    l_i = lax.broadcasted_iota(i32, (nch, d), 1)
    diag = (jnp.bitwise_and(l_i, nch - 1) == c_i).astype(f32)
    masked = (res_all[...].reshape(tt, nch, d) * diag).reshape(tt * nch, d)
    grp = (jnp.right_shift(lax.broadcasted_iota(i32, (d, nk), 0), sh)
           == lax.broadcasted_iota(i32, (d, nk), 1)).astype(f32).astype(bf16)
    mhi, mlo = _split_bf16(masked)
    g = jnp.dot(mhi, grp, preferred_element_type=f32) + jnp.dot(mlo, grp, preferred_element_type=f32)
    sel = (jnp.right_shift(lax.broadcasted_iota(i32, (tt, tt * nch), 1), sh)
           == lax.broadcasted_iota(i32, (tt, tt * nch), 0)).astype(f32).astype(bf16)
    ghi, glo = _split_bf16(g)
    act = jnp.dot(sel, ghi, preferred_element_type=f32) + jnp.dot(sel, glo, preferred_element_type=f32)
    w_ref[...] = gate_ref[...] * _gelu(act)


def _peer_v_kernel(idx_ref, w_ref, tab_ref, o_ref, whi, wlo):
    tt, nk = w_ref.shape
    nch = o_ref.shape[1]
    rows = nch // 2
    d = nch * LANES
    sh = int(math.log2(nch))
    nun = PEER_UNROLL
    k_i = lax.broadcasted_iota(i32, (nk, d), 0)
    n_i = lax.broadcasted_iota(i32, (nk, d), 1)
    expand = (jnp.right_shift(n_i, sh) == k_i).astype(f32).astype(bf16)
    hi, lo = _split_bf16(w_ref[...])
    whi[...] = jnp.dot(hi, expand, preferred_element_type=f32)
    wlo[...] = jnp.dot(lo, expand, preferred_element_type=f32)
    c_i = lax.broadcasted_iota(i32, (nch, d), 0)
    l_i = lax.broadcasted_iota(i32, (nch, d), 1)
    diag = (jnp.bitwise_and(l_i, nch - 1) == c_i).astype(f32)

    def body(i, carry):
        ts = [i * nun + u for u in range(nun)]
        ms = _gather_rows(idx_ref, ts, tab_ref, nk, rows)
        for t, m in zip(ts, ms):
            lhs = jnp.concatenate([whi[pl.ds(t, 1), :] * diag, wlo[pl.ds(t, 1), :] * diag], axis=0).astype(bf16)
            out = jnp.dot(lhs, m, preferred_element_type=f32)
            o_ref[t] = out[:nch] + out[nch:]
        return carry

    lax.fori_loop(0, tt // nun, body, 0)


def _peer_u(idx, h2r, gate, tab, tt):
    n, nk = gate.shape
    nch = h2r.shape[0] // n
    h3 = h2r.reshape(n, nch, LANES)
    return pl.pallas_call(
        _peer_u_kernel,
        grid=(n // tt,),
        in_specs=[pl.BlockSpec((tt, nk), lambda i: (i, 0), memory_space=pltpu.SMEM),
                  pl.BlockSpec((tt, nch, LANES), lambda i: (i, 0, 0)),
                  pl.BlockSpec((tt, nk), lambda i: (i, 0)),
                  pl.BlockSpec(tab.shape, lambda i: (0, 0), pipeline_mode=pl.Buffered(1))],
        out_specs=pl.BlockSpec((tt, nk), lambda i: (i, 0)),
        out_shape=jax.ShapeDtypeStruct((n, nk), f32),
        scratch_shapes=[pltpu.VMEM((tt * nch, nch * LANES), f32)],
        compiler_params=pltpu.CompilerParams(
            dimension_semantics=("parallel",), vmem_limit_bytes=VMEM_LIMIT),
        name="peer_u",
    )(idx, h3, gate, tab)


def _peer_v(idx, w, tab, nch, tt):
    n, nk = w.shape
    d = nch * LANES
    return pl.pallas_call(
        _peer_v_kernel,
        grid=(n // tt,),
        in_specs=[pl.BlockSpec((tt, nk), lambda i: (i, 0), memory_space=pltpu.SMEM),
                  pl.BlockSpec((tt, nk), lambda i: (i, 0)),
                  pl.BlockSpec(tab.shape, lambda i: (0, 0), pipeline_mode=pl.Buffered(1))],
        out_specs=pl.BlockSpec((tt, nch, LANES), lambda i: (i, 0, 0)),
        out_shape=jax.ShapeDtypeStruct((n, nch, LANES), f32),
        scratch_shapes=[pltpu.VMEM((tt, d), f32), pltpu.VMEM((tt, d), f32)],
        compiler_params=pltpu.CompilerParams(
            dimension_semantics=("parallel",), vmem_limit_bytes=VMEM_LIMIT),
        name="peer_v",
    )(idx, w, tab)


def _final_kernel(x1_ref, p_ref, mod_ref, g_ref, o_ref):
    tm, d = x1_ref.shape
    nch = d // LANES
    peer = jnp.concatenate([p_ref[pl.ds(c, tm, stride=nch), :] for c in range(nch)], axis=1)
    x2 = x1_ref[...] + mod_ref[5:6, :] * peer
    o_ref[...] = _rms(x2, g_ref[...])


def _final(x1, peer2d, mod3, g_final, bsz, seq, tm):
    n, d = x1.shape
    nt = seq // tm
    nch = d // LANES
    return pl.pallas_call(
        _final_kernel,
        grid=(bsz, nt),
        in_specs=[pl.BlockSpec((tm, d), lambda b, i: (b * nt + i, 0)),
                  pl.BlockSpec((tm * nch, LANES), lambda b, i: (b * nt + i, 0)),
                  pl.BlockSpec((None, N_MOD, d), lambda b, i: (b, 0, 0)),
                  pl.BlockSpec((1, d), lambda b, i: (0, 0))],
        out_specs=pl.BlockSpec((None, tm, d), lambda b, i: (b, i, 0)),
        out_shape=jax.ShapeDtypeStruct((bsz, seq, d), f32),
        compiler_params=pltpu.CompilerParams(
            dimension_semantics=("parallel", "parallel"), vmem_limit_bytes=VMEM_LIMIT),
        name="final",
    )(x1, peer2d, mod3, g_final.reshape(1, d))


def kernel(x, c, w_ada, b_ada, g_mix, w_in, sgu_ln_g, sgu_ln_b, w_s, b_s, ssm_a_re, ssm_a_im, ssm_log_dt, ssm_b_re, ssm_b_im, ssm_c_re, ssm_c_im, ssm_d, w_glu, b_glu, w_out, g_ffn, w_q, peer_keys, peer_u, peer_v, g_final):
    bsz, seq, d = x.shape
    depth = w_ada.shape[0]
    assert depth == 1 and d % LANES == 0 and seq % CHUNK == 0
    tm_in = min(seq, 512)
    tm_out = min(seq, 256)
    tt_scan = min(seq, 64)
    tt_peer = 128
    n = bsz * seq
    nch = d // LANES

    l = 0
    mod3 = _ada(c, w_ada[l], b_ada[l]).reshape(bsz, N_MOD, d)
    ygm, zs = _mix_in(x, mod3, g_mix[l], w_in[l], sgu_ln_g[l], sgu_ln_b[l], w_s[l], b_s[l], tm_in)
    abr, abi, bbr, bbi = _s5_params(ssm_a_re[l], ssm_a_im[l], ssm_log_dt[l], ssm_b_re[l], ssm_b_im[l])
    yss = _s5(zs, abr[:, 0, :], abi[:, 0, :], bbr, bbi, ssm_c_re[l], ssm_c_im[l],
              ssm_d[l], w_glu[l], b_glu[l], tt_scan)
    x1, h2r, idx, gate = _mix_out(x, ygm, yss, mod3, w_out[l], g_ffn[l], w_q[l], peer_keys[l], tm_out)
    w = _peer_u(idx, h2r, gate, _pack_table(peer_u[l]), tt_peer)
    peer = _peer_v(idx, w, _pack_table(peer_v[l]), nch, tt_peer)
    return _final(x1, peer.reshape(n * nch, LANES), mod3, g_final, bsz, seq, tm_in)
```

```python
import functools
import math

import jax
import jax.numpy as jnp
from jax import lax
from jax.experimental import pallas as pl
from jax.experimental.pallas import tpu as pltpu

f32 = jnp.float32
bf16 = jnp.bfloat16
i32 = jnp.int32

EPS = 1e-6
LANES = 128
SUBLANES = 8
GM_HEADS = 4
CHUNK = 128
SSM_GROUPS = 32
SSM_GROUP = 16
SSM_STATE = 64
PEER_HEADS = 8
PEER_NKEYS = 128
PEER_TOPK = 16
PEER_DHALF = 64
N_MOD = 6
VMEM_LIMIT = 48 * 1024 * 1024
PEER_UNROLL = 32

_HIGHEST = lax.Precision.HIGHEST
_NT = (((1,), (1,)), ((), ()))


def _gelu(x):
    return 0.5 * x * (1.0 + jnp.tanh(0.7978845608028654 * (x + 0.044715 * (x * x * x))))


def _rms(x, g):
    return x * lax.rsqrt(jnp.mean(x * x, axis=-1, keepdims=True) + EPS) * g


def _ada_kernel(c_ref, w_ref, b_ref, o_ref):
    c = c_ref[...]
    cond = c * jax.nn.sigmoid(c)
    o_ref[...] = jnp.dot(cond, w_ref[...], precision=_HIGHEST, preferred_element_type=f32) + b_ref[...]


def _ada(c, w, b):
    bsz, d = c.shape
    n = w.shape[1]
    tn = 1536
    return pl.pallas_call(
        _ada_kernel,
        grid=(n // tn,),
        in_specs=[pl.BlockSpec((bsz, d), lambda j: (0, 0)),
                  pl.BlockSpec((d, tn), lambda j: (0, j)),
                  pl.BlockSpec((1, tn), lambda j: (0, j))],
        out_specs=pl.BlockSpec((bsz, tn), lambda j: (0, j)),
        out_shape=jax.ShapeDtypeStruct((bsz, n), f32),
        compiler_params=pltpu.CompilerParams(vmem_limit_bytes=VMEM_LIMIT),
        name="ada",
    )(c, w, b.reshape(1, n))


def _s5_params_kernel(are_ref, aim_ref, ldt_ref, bre_ref, bim_ref,
                      abr_ref, abi_ref, bbr_ref, bbi_ref):
    lre = jnp.minimum(are_ref[...], -1e-4)
    lim = aim_ref[...]
    dt = jnp.exp(ldt_ref[...])
    mag = jnp.exp(lre * dt)
    abr = mag * jnp.cos(lim * dt)
    abi = mag * jnp.sin(lim * dt)
    abr_ref[...] = abr
    abi_ref[...] = abi
    nr = abr - 1.0
    ni = abi
    den = lre * lre + lim * lim
    fr = (nr * lre + ni * lim) / den
    fi = (ni * lre - nr * lim) / den
    bre = bre_ref[...]
    bim = bim_ref[...]
    bbr_ref[...] = fr * bre - fi * bim
    bbi_ref[...] = fr * bim + fi * bre


def _s5_params(a_re, a_im, log_dt, b_re, b_im):
    g, p = a_re.shape
    hg = b_re.shape[2]
    are = a_re.reshape(g, 1, p)
    aim = a_im.reshape(g, 1, p)
    ldt = jnp.broadcast_to(log_dt.reshape(g, 1, 1), (g, 1, p))
    bre = jnp.transpose(b_re, (0, 2, 1))
    bim = jnp.transpose(b_im, (0, 2, 1))
    small = jax.ShapeDtypeStruct((g, 1, p), f32)
    big = jax.ShapeDtypeStruct((g, hg, p), f32)
    return pl.pallas_call(
        _s5_params_kernel,
        out_shape=(small, small, big, big),
        name="s5_params",
    )(are, aim, ldt, bre, bim)


def _mix_in_kernel(x_ref, mod_ref, g_ref, win_ref, lng_ref, lnb_ref, ws_ref, bs_ref,
                   ygm_ref, zs_ref):
    tm = x_ref.shape[0]
    gw = ygm_ref.shape[1]
    x = x_ref[...]
    h = _rms(x, g_ref[...]) * (1.0 + mod_ref[1:2, :]) + mod_ref[0:1, :]
    z = jnp.dot(h.astype(bf16), win_ref[...], preferred_element_type=f32)
    zs_ref[...] = z[:, 2 * gw:]
    u = _gelu(z[:, :gw])
    gv = _gelu(z[:, gw:2 * gw])
    mu = jnp.mean(gv, axis=-1, keepdims=True)
    dv = gv - mu
    var = jnp.mean(dv * dv, axis=-1, keepdims=True)
    v = (dv * lax.rsqrt(var + EPS) * lng_ref[...] + lnb_ref[...]).astype(bf16)
    row = lax.broadcasted_iota(i32, (CHUNK, CHUNK), 0)
    col = lax.broadcasted_iota(i32, (CHUNK, CHUNK), 1)
    causal = row >= col
    hd_w = gw // GM_HEADS
    for hd in range(GM_HEADS):
        wm = jnp.where(causal, ws_ref[hd], 0.0).astype(bf16)
        bias = bs_ref[hd]
        for ck in range(tm // CHUNK):
            rs = slice(ck * CHUNK, (ck + 1) * CHUNK)
            cs = slice(hd * hd_w, (hd + 1) * hd_w)
            mixed = jnp.dot(wm, v[rs, cs], preferred_element_type=f32) + bias
            ygm_ref[rs, cs] = (u[rs, cs] * mixed).astype(bf16)


def _mix_in(x, mod3, g_mix, w_in, ln_g, ln_b, w_s, b_s, tm):
    bsz, seq, d = x.shape
    gw = ln_g.shape[0]
    nin = w_in.shape[1]
    bs_b = jnp.broadcast_to(b_s[:, :, None], (GM_HEADS, CHUNK, gw // GM_HEADS))
    return pl.pallas_call(
        _mix_in_kernel,
        grid=(bsz, seq // tm),
        in_specs=[pl.BlockSpec((None, tm, d), lambda b, i: (b, i, 0)),
                  pl.BlockSpec((None, N_MOD, d), lambda b, i: (b, 0, 0)),
                  pl.BlockSpec((1, d), lambda b, i: (0, 0)),
                  pl.BlockSpec((d, nin), lambda b, i: (0, 0)),
                  pl.BlockSpec((1, gw), lambda b, i: (0, 0)),
                  pl.BlockSpec((1, gw), lambda b, i: (0, 0)),
                  pl.BlockSpec((GM_HEADS, CHUNK, CHUNK), lambda b, i: (0, 0, 0)),
                  pl.BlockSpec((GM_HEADS, CHUNK, gw // GM_HEADS), lambda b, i: (0, 0, 0))],
        out_specs=[pl.BlockSpec((None, tm, gw), lambda b, i: (b, i, 0)),
                   pl.BlockSpec((None, tm, nin - 2 * gw), lambda b, i: (b, i, 0))],
        out_shape=(jax.ShapeDtypeStruct((bsz, seq, gw), bf16),
                   jax.ShapeDtypeStruct((bsz, seq, nin - 2 * gw), f32)),
        compiler_params=pltpu.CompilerParams(
            dimension_semantics=("parallel", "parallel"), vmem_limit_bytes=VMEM_LIMIT),
        name="mix_in",
    )(x, mod3, g_mix.reshape(1, d), w_in.astype(bf16), ln_g.reshape(1, gw), ln_b.reshape(1, gw), w_s, bs_b)


def _s5_kernel(zs_ref, wb_ref, av_ref, wc_ref, dsk_ref, wglu_ref, bglu_ref, o_ref,
               zi, st, carry, yo):
    bsz, tt, w = zs_ref.shape
    nlt = w // LANES
    hw = st.shape[2] // 2
    nhalf = st.shape[0]

    @pl.when(pl.program_id(0) == 0)
    def _():
        carry[...] = jnp.zeros_like(carry)

    for b in range(bsz):
        zb = zs_ref[b]
        for j in range(nlt):
            zi[j, pl.ds(b, tt, stride=bsz), :] = zb[:, j * LANES:(j + 1) * LANES]

    ys = []
    lt_per_half = nlt // nhalf
    for hf in range(nhalf):
        zh = jnp.concatenate([zi[hf * lt_per_half + j] for j in range(lt_per_half)], axis=1)
        st[hf] = jnp.dot(zh.astype(bf16), wb_ref[hf], preferred_element_type=f32)
        ar = jnp.broadcast_to(av_ref[2 * hf:2 * hf + 1, :], (bsz, hw))
        ai = jnp.broadcast_to(av_ref[2 * hf + 1:2 * hf + 2, :], (bsz, hw))

        def step(t, xr, xi):
            r0 = pl.multiple_of(t * bsz, bsz)
            bur = st[hf, pl.ds(r0, bsz), 0:hw]
            bui = st[hf, pl.ds(r0, bsz), hw:2 * hw]
            nxr = ar * xr - ai * xi + bur
            nxi = ar * xi + ai * xr + bui
            st[hf, pl.ds(r0, bsz), 0:hw] = nxr
            st[hf, pl.ds(r0, bsz), hw:2 * hw] = nxi
            return nxr, nxi

        def body(t2, c):
            xr, xi = c
            xr, xi = step(2 * t2, xr, xi)
            xr, xi = step(2 * t2 + 1, xr, xi)
            return xr, xi

        xr, xi = lax.fori_loop(0, tt // 2, body, (carry[hf, :, 0:hw], carry[hf, :, hw:2 * hw]))
        carry[hf, :, 0:hw] = xr
        carry[hf, :, hw:2 * hw] = xi
        ys.append(jnp.dot(st[hf].astype(bf16), wc_ref[hf], preferred_element_type=f32))

    u = jnp.concatenate([zi[j] for j in range(nlt)], axis=1)
    y = jnp.concatenate(ys, axis=1) + dsk_ref[...] * u
    y = _gelu(y)
    gate = jax.nn.sigmoid(jnp.dot(y.astype(bf16), wglu_ref[...], preferred_element_type=f32) + bglu_ref[...])
    y = y * gate
    for j in range(nlt):
        yo[j] = y[:, j * LANES:(j + 1) * LANES]
    for b in range(bsz):
        o_ref[b] = jnp.concatenate(
            [yo[j, pl.ds(b, tt, stride=bsz), :] for j in range(nlt)], axis=1).astype(bf16)


def _s5(zs, abr, abi, bbr, bbi, c_re, c_im, d_skip, w_glu, b_glu, tt):
    bsz, seq, w = zs.shape
    g, hg, p = bbr.shape
    nhalf = 2
    gh = g // nhalf
    eye = jnp.eye(gh, dtype=f32)

    def blockdiag_in(m):
        return jnp.einsum('ghp,gk->ghkp', m, eye).reshape(gh * hg, gh * p)

    def blockdiag_out(m):
        return jnp.einsum('ghp,gk->gpkh', m, eye).reshape(gh * p, gh * hg)

    wb = jnp.stack([jnp.concatenate([blockdiag_in(bbr[h * gh:(h + 1) * gh]),
                                     blockdiag_in(bbi[h * gh:(h + 1) * gh])], axis=1)
                    for h in range(nhalf)]).astype(bf16)
    wc = jnp.stack([jnp.concatenate([blockdiag_out(c_re[h * gh:(h + 1) * gh]),
                                     blockdiag_out(-c_im[h * gh:(h + 1) * gh])], axis=0)
                    for h in range(nhalf)]).astype(bf16)
    av = jnp.concatenate([jnp.stack([abr[h * gh:(h + 1) * gh].reshape(gh * p),
                                     abi[h * gh:(h + 1) * gh].reshape(gh * p)])
                          for h in range(nhalf)], axis=0)
    hw = gh * p
    nlt = w // LANES
    return pl.pallas_call(
        _s5_kernel,
        grid=(seq // tt,),
        in_specs=[pl.BlockSpec((bsz, tt, w), lambda i: (0, i, 0)),
                  pl.BlockSpec(wb.shape, lambda i: (0, 0, 0)),
                  pl.BlockSpec(av.shape, lambda i: (0, 0)),
                  pl.BlockSpec(wc.shape, lambda i: (0, 0, 0)),
                  pl.BlockSpec((1, w), lambda i: (0, 0)),
                  pl.BlockSpec((w, w), lambda i: (0, 0)),
                  pl.BlockSpec((1, w), lambda i: (0, 0))],
        out_specs=pl.BlockSpec((bsz, tt, w), lambda i: (0, i, 0)),
        out_shape=jax.ShapeDtypeStruct((bsz, seq, w), bf16),
        scratch_shapes=[pltpu.VMEM((nlt, tt * bsz, LANES), f32),
                        pltpu.VMEM((nhalf, tt * bsz, 2 * hw), f32),
                        pltpu.VMEM((nhalf, bsz, 2 * hw), f32),
                        pltpu.VMEM((nlt, tt * bsz, LANES), f32)],
        compiler_params=pltpu.CompilerParams(
            dimension_semantics=("arbitrary",), vmem_limit_bytes=VMEM_LIMIT),
        name="s5",
    )(zs, wb, av, wc, d_skip.reshape(1, w), w_glu.astype(bf16), b_glu.reshape(1, w))


def _top16(vals, rowid):
    nrow = vals.shape[0]
    tops, topi = [], []
    for _ in range(PEER_TOPK):
        m = jnp.max(vals, axis=0, keepdims=True)
        am = jnp.min(jnp.where(vals == m, rowid, float(nrow)), axis=0, keepdims=True)
        tops.append(m)
        topi.append(am)
        vals = jnp.where(rowid == am, -jnp.inf, vals)
    return tops, topi


def _route_head(keys2, qt2):
    tm = qt2.shape[1]
    keyid = lax.broadcasted_iota(i32, (PEER_NKEYS, tm), 0).astype(f32)
    ncand = 16 + 7 * 8 + 8
    prow = lax.broadcasted_iota(i32, (ncand, tm), 0)
    pos = jnp.where(prow < 16, prow,
                    jnp.where(prow < 72, jnp.right_shift(prow - 8, 3) * 16 + jnp.bitwise_and(prow, 7),
                              (prow - 64) * 16)).astype(f32)
    halves = []
    for c in range(2):
        s = jnp.dot(keys2[c], qt2[c * PEER_DHALF:(c + 1) * PEER_DHALF, :], preferred_element_type=f32)
        halves.append(_top16(s, keyid))
    (ta, ia), (tb, ib) = halves
    b16 = jnp.concatenate(tb, axis=0)
    ib16 = jnp.concatenate(ib, axis=0)
    b8, ib8 = b16[:8], ib16[:8]
    cand = [ta[0] + b16]
    cid = [ia[0] * float(PEER_NKEYS) + ib16]
    for i in range(1, 8):
        cand.append(ta[i] + b8)
        cid.append(ia[i] * float(PEER_NKEYS) + ib8)
    a_hi = jnp.concatenate(ta[8:], axis=0)
    ia_hi = jnp.concatenate(ia[8:], axis=0)
    cand.append(a_hi + tb[0])
    cid.append(ia_hi * float(PEER_NKEYS) + ib[0])
    cand = jnp.concatenate(cand, axis=0)
    nexp = float(PEER_NKEYS * PEER_NKEYS)
    key = pos * nexp + jnp.concatenate(cid, axis=0)
    best, eid = [], []
    for _ in range(PEER_TOPK):
        m = jnp.max(cand, axis=0, keepdims=True)
        kmin = jnp.min(jnp.where(cand == m, key, 1e9), axis=0, keepdims=True)
        eid.append(kmin - jnp.floor(kmin * (1.0 / nexp)) * nexp)
        best.append(m)
        cand = jnp.where(key == kmin, -jnp.inf, cand)
    best = jnp.concatenate(best, axis=0)
    e = jnp.exp(best - best[0:1])
    return e / jnp.sum(e, axis=0, keepdims=True), jnp.concatenate(eid, axis=0)


def _store_routes(gate_t, id_t, idx_ref, gate_ref):
    gate_ref[...] = gate_t.T
    idx_ref[...] = (id_t.T * 4.0).astype(i32)


def _mix_out_kernel(x_ref, ygm_ref, yss_ref, mod_ref, wo1_ref, wo2_ref, g_ref, wqt_ref, keys_ref,
                    x1_ref, h2_ref, idx_ref, gate_ref):
    tm, d = x_ref.shape
    y = jnp.dot(ygm_ref[...], wo1_ref[...], preferred_element_type=f32)
    y = y + jnp.dot(yss_ref[...], wo2_ref[...], preferred_element_type=f32)
    x1 = x_ref[...] + mod_ref[2:3, :] * y
    x1_ref[...] = x1
    h2 = _rms(x1, g_ref[...]) * (1.0 + mod_ref[4:5, :]) + mod_ref[3:4, :]
    for c in range(d // LANES):
        h2_ref[pl.ds(c, tm, stride=d // LANES), :] = h2[:, c * LANES:(c + 1) * LANES]
    qt = lax.dot_general(wqt_ref[...], h2.astype(bf16), _NT, preferred_element_type=f32).astype(bf16)
    dk = 2 * PEER_DHALF
    routes = [_route_head(keys_ref[hd], qt[hd * dk:(hd + 1) * dk, :]) for hd in range(PEER_HEADS)]
    _store_routes(jnp.concatenate([g for g, _ in routes], axis=0),
                  jnp.concatenate([e for _, e in routes], axis=0), idx_ref, gate_ref)


def _mix_out(x, ygm, yss, mod3, w_out, g_ffn, w_q, keys, tm):
    bsz, seq, d = x.shape
    gw = ygm.shape[2]
    nt = seq // tm
    n = bsz * seq
    dq = w_q.shape[1]
    nk = PEER_HEADS * PEER_TOPK
    wo = w_out.astype(bf16)
    return pl.pallas_call(
        _mix_out_kernel,
        grid=(bsz, nt),
        in_specs=[pl.BlockSpec((None, tm, d), lambda b, i: (b, i, 0)),
                  pl.BlockSpec((None, tm, gw), lambda b, i: (b, i, 0)),
                  pl.BlockSpec((None, tm, gw), lambda b, i: (b, i, 0)),
                  pl.BlockSpec((None, N_MOD, d), lambda b, i: (b, 0, 0)),
                  pl.BlockSpec((gw, d), lambda b, i: (0, 0)),
                  pl.BlockSpec((gw, d), lambda b, i: (0, 0)),
                  pl.BlockSpec((1, d), lambda b, i: (0, 0)),
                  pl.BlockSpec((dq, d), lambda b, i: (0, 0)),
                  pl.BlockSpec(keys.shape, lambda b, i: (0, 0, 0, 0))],
        out_specs=[pl.BlockSpec((tm, d), lambda b, i: (b * nt + i, 0)),
                   pl.BlockSpec((tm * (d // LANES), LANES), lambda b, i: (b * nt + i, 0)),
                   pl.BlockSpec((tm, nk), lambda b, i: (b * nt + i, 0)),
                   pl.BlockSpec((tm, nk), lambda b, i: (b * nt + i, 0))],
        out_shape=(jax.ShapeDtypeStruct((n, d), f32),
                   jax.ShapeDtypeStruct((n * (d // LANES), LANES), f32),
                   jax.ShapeDtypeStruct((n, nk), i32),
                   jax.ShapeDtypeStruct((n, nk), f32)),
        compiler_params=pltpu.CompilerParams(
            dimension_semantics=("parallel", "parallel"), vmem_limit_bytes=VMEM_LIMIT),
        name="mix_out",
    )(x, ygm, yss, mod3, wo[:gw], wo[gw:], g_ffn.reshape(1, d), w_q.T.astype(bf16), keys.astype(bf16))


def _pack_table_kernel(t_ref, o_ref):
    eb, d = t_ref.shape
    rows = d // (2 * LANES)
    x = t_ref[...].astype(bf16).astype(f32)
    bits = lax.bitcast_convert_type(x, i32)
    for s in range(rows):
        lo = lax.shift_right_logical(bits[:, 2 * s * LANES:(2 * s + 1) * LANES], 16)
        hi = jnp.bitwise_and(bits[:, (2 * s + 1) * LANES:(2 * s + 2) * LANES], -65536)
        o_ref[pl.ds(s, eb, stride=rows), :] = jnp.bitwise_or(lo, hi)


def _pack_table(tab):
    e, d = tab.shape
    rows = d // (2 * LANES)
    eb = 512
    return pl.pallas_call(
        _pack_table_kernel,
        grid=(e // eb,),
        in_specs=[pl.BlockSpec((eb, d), lambda i: (i, 0))],
        out_specs=pl.BlockSpec((eb * rows, LANES), lambda i: (i, 0)),
        out_shape=jax.ShapeDtypeStruct((e * rows, LANES), i32),
        compiler_params=pltpu.CompilerParams(
            dimension_semantics=("parallel",), vmem_limit_bytes=VMEM_LIMIT),
        name="pack_table",
    )(tab)


def _gather_rows(idx_ref, ts, tab_ref, nk, rows):
    row_refs = [idx_ref.at[pl.ds(t, 1)] for t in ts]
    out = []
    for row_ref in row_refs:
        slabs = [tab_ref[pl.ds(pl.multiple_of(row_ref[0, k], rows), rows), :] for k in range(nk)]
        out.append(pltpu.bitcast(jnp.concatenate(slabs, axis=0), bf16))
    return out


def _split_bf16(x):
    hi = x.astype(bf16)
    return hi, (x - hi.astype(f32)).astype(bf16)


def _peer_u_kernel(idx_ref, h_ref, gate_ref, tab_ref, w_ref, res_all):
    tt, nk = gate_ref.shape
    nch = h_ref.shape[1]
    rows = nch // 2
    d = nch * LANES
    sh = int(math.log2(nch))
    nun = PEER_UNROLL

    def body(i, carry):
        ts = [i * nun + u for u in range(nun)]
        ms = _gather_rows(idx_ref, ts, tab_ref, nk, rows)
        for t, m in zip(ts, ms):
            hb = h_ref[t].astype(bf16)
            r0 = pl.multiple_of(t * nch, nch)
            res_all[pl.ds(r0, nch), :] = lax.dot_general(hb, m, _NT, preferred_element_type=f32)
        return carry

    lax.fori_loop(0, tt // nun, body, 0)

    c_i = lax.broadcasted_iota(i32, (nch, d), 0)
    l_i = lax.broadcasted_iota(i32, (nch, d), 1)
    diag = (jnp.bitwise_and(l_i, nch - 1) == c_i).astype(f32)
    masked = (res_all[...].reshape(tt, nch, d) * diag).reshape(tt * nch, d)
    grp = (jnp.right_shift(lax.broadcasted_iota(i32, (d, nk), 0), sh)
           == lax.broadcasted_iota(i32, (d, nk), 1)).astype(f32).astype(bf16)
    mhi, mlo = _split_bf16(masked)
    g = jnp.dot(mhi, grp, preferred_element_type=f32) + jnp.dot(mlo, grp, preferred_element_type=f32)
    sel = (jnp.right_shift(lax.broadcasted_iota(i32, (tt, tt * nch), 1), sh)
           == lax.broadcasted_iota(i32, (tt, tt * nch), 0)).astype(f32).astype(bf16)
    ghi, glo = _split_bf16(g)
    act = jnp.dot(sel, ghi, preferred_element_type=f32) + jnp.dot(sel, glo, preferred_element_type=f32)
    w_ref[...] = gate_ref[...] * _gelu(act)


def _peer_v_kernel(idx_ref, w_ref, tab_ref, o_ref, whi, wlo):
    tt, nk = w_ref.shape
    nch = o_ref.shape[1]
    rows = nch // 2
    d = nch * LANES
    sh = int(math.log2(nch))
    nun = PEER_UNROLL
    k_i = lax.broadcasted_iota(i32, (nk, d), 0)
    n_i = lax.broadcasted_iota(i32, (nk, d), 1)
    expand = (jnp.right_shift(n_i, sh) == k_i).astype(f32).astype(bf16)
    hi, lo = _split_bf16(w_ref[...])
    whi[...] = jnp.dot(hi, expand, preferred_element_type=f32)
    wlo[...] = jnp.dot(lo, expand, preferred_element_type=f32)
    c_i = lax.broadcasted_iota(i32, (nch, d), 0)
    l_i = lax.broadcasted_iota(i32, (nch, d), 1)
    diag = (jnp.bitwise_and(l_i, nch - 1) == c_i).astype(f32)

    def body(i, carry):
        ts = [i * nun + u for u in range(nun)]
        ms = _gather_rows(idx_ref, ts, tab_ref, nk, rows)
        for t, m in zip(ts, ms):
            lhs = jnp.concatenate([whi[pl.ds(t, 1), :] * diag, wlo[pl.ds(t, 1), :] * diag], axis=0).astype(bf16)
            out = jnp.dot(lhs, m, preferred_element_type=f32)
            o_ref[t] = out[:nch] + out[nch:]
        return carry

    lax.fori_loop(0, tt // nun, body, 0)


def _peer_u(idx, h2r, gate, tab, tt):
    n, nk = gate.shape
    nch = h2r.shape[0] // n
    h3 = h2r.reshape(n, nch, LANES)
    return pl.pallas_call(
        _peer_u_kernel,
        grid=(n // tt,),
        in_specs=[pl.BlockSpec((tt, nk), lambda i: (i, 0), memory_space=pltpu.SMEM),
                  pl.BlockSpec((tt, nch, LANES), lambda i: (i, 0, 0)),
                  pl.BlockSpec((tt, nk), lambda i: (i, 0)),
                  pl.BlockSpec(tab.shape, lambda i: (0, 0), pipeline_mode=pl.Buffered(1))],
        out_specs=pl.BlockSpec((tt, nk), lambda i: (i, 0)),
        out_shape=jax.ShapeDtypeStruct((n, nk), f32),
        scratch_shapes=[pltpu.VMEM((tt * nch, nch * LANES), f32)],
        compiler_params=pltpu.CompilerParams(
            dimension_semantics=("parallel",), vmem_limit_bytes=VMEM_LIMIT),
        name="peer_u",
    )(idx, h3, gate, tab)


def _peer_v(idx, w, tab, nch, tt):
    n, nk = w.shape
    d = nch * LANES
    return pl.pallas_call(
        _peer_v_kernel,
        grid=(n // tt,),
        in_specs=[pl.BlockSpec((tt, nk), lambda i: (i, 0), memory_space=pltpu.SMEM),
                  pl.BlockSpec((tt, nk), lambda i: (i, 0)),
                  pl.BlockSpec(tab.shape, lambda i: (0, 0), pipeline_mode=pl.Buffered(1))],
        out_specs=pl.BlockSpec((tt, nch, LANES), lambda i: (i, 0, 0)),
        out_shape=jax.ShapeDtypeStruct((n, nch, LANES), f32),
        scratch_shapes=[pltpu.VMEM((tt, d), f32), pltpu.VMEM((tt, d), f32)],
        compiler_params=pltpu.CompilerParams(
            dimension_semantics=("parallel",), vmem_limit_bytes=VMEM_LIMIT),
        name="peer_v",
    )(idx, w, tab)


def _final_kernel(x1_ref, p_ref, mod_ref, g_ref, o_ref):
    tm, d = x1_ref.shape
    nch = d // LANES
    peer = jnp.concatenate([p_ref[pl.ds(c, tm, stride=nch), :] for c in range(nch)], axis=1)
    x2 = x1_ref[...] + mod_ref[5:6, :] * peer
    o_ref[...] = _rms(x2, g_ref[...])


def _final(x1, peer2d, mod3, g_final, bsz, seq, tm):
    n, d = x1.shape
    nt = seq // tm
    nch = d // LANES
    return pl.pallas_call(
        _final_kernel,
        grid=(bsz, nt),
        in_specs=[pl.BlockSpec((tm, d), lambda b, i: (b * nt + i, 0)),
                  pl.BlockSpec((tm * nch, LANES), lambda b, i: (b * nt + i, 0)),
                  pl.BlockSpec((None, N_MOD, d), lambda b, i: (b, 0, 0)),
                  pl.BlockSpec((1, d), lambda b, i: (0, 0))],
        out_specs=pl.BlockSpec((None, tm, d), lambda b, i: (b, i, 0)),
        out_shape=jax.ShapeDtypeStruct((bsz, seq, d), f32),
        compiler_params=pltpu.CompilerParams(
            dimension_semantics=("parallel", "parallel"), vmem_limit_bytes=VMEM_LIMIT),
        name="final",
    )(x1, peer2d, mod3, g_final.reshape(1, d))


def kernel(x, c, w_ada, b_ada, g_mix, w_in, sgu_ln_g, sgu_ln_b, w_s, b_s, ssm_a_re, ssm_a_im, ssm_log_dt, ssm_b_re, ssm_b_im, ssm_c_re, ssm_c_im, ssm_d, w_glu, b_glu, w_out, g_ffn, w_q, peer_keys, peer_u, peer_v, g_final):
    bsz, seq, d = x.shape
    depth = w_ada.shape[0]
    assert depth == 1 and d % LANES == 0 and seq % CHUNK == 0
    tm_in = min(seq, 512)
    tm_out = min(seq, 256)
    tt_scan = min(seq, 64)
    tt_peer = 128
    n = bsz * seq
    nch = d // LANES

    l = 0
    mod3 = _ada(c, w_ada[l], b_ada[l]).reshape(bsz, N_MOD, d)
    ygm, zs = _mix_in(x, mod3, g_mix[l], w_in[l], sgu_ln_g[l], sgu_ln_b[l], w_s[l], b_s[l], tm_in)
    abr, abi, bbr, bbi = _s5_params(ssm_a_re[l], ssm_a_im[l], ssm_log_dt[l], ssm_b_re[l], ssm_b_im[l])
    yss = _s5(zs, abr[:, 0, :], abi[:, 0, :], bbr, bbi, ssm_c_re[l], ssm_c_im[l],
              ssm_d[l], w_glu[l], b_glu[l], tt_scan)
    x1, h2r, idx, gate = _mix_out(x, ygm, yss, mod3, w_out[l], g_ffn[l], w_q[l], peer_keys[l], tm_out)
    w = _peer_u(idx, h2r, gate, _pack_table(peer_u[l]), tt_peer)
    peer = _peer_v(idx, w, _pack_table(peer_v[l]), nch, tt_peer)
    return _final(x1, peer.reshape(n * nch, LANES), mod3, g_final, bsz, seq, tm_in)
```

```python
import functools
import math

import jax
import jax.numpy as jnp
from jax import lax
from jax.experimental import pallas as pl
from jax.experimental.pallas import tpu as pltpu

f32 = jnp.float32
bf16 = jnp.bfloat16
i32 = jnp.int32

EPS = 1e-6
LANES = 128
SUBLANES = 8
GM_HEADS = 4
CHUNK = 128
SSM_GROUPS = 32
SSM_GROUP = 16
SSM_STATE = 64
PEER_HEADS = 8
PEER_NKEYS = 128
PEER_TOPK = 16
PEER_DHALF = 64
N_MOD = 6
VMEM_LIMIT = 48 * 1024 * 1024
PEER_UNROLL = 32

_HIGHEST = lax.Precision.HIGHEST
_NT = (((1,), (1,)), ((), ()))


def _gelu(x):
    return 0.5 * x * (1.0 + jnp.tanh(0.7978845608028654 * (x + 0.044715 * (x * x * x))))


def _rms(x, g):
    return x * lax.rsqrt(jnp.mean(x * x, axis=-1, keepdims=True) + EPS) * g


def _ada_kernel(c_ref, w_ref, b_ref, o_ref):
    c = c_ref[...]
    cond = c * jax.nn.sigmoid(c)
    o_ref[...] = jnp.dot(cond, w_ref[...], precision=_HIGHEST, preferred_element_type=f32) + b_ref[...]


def _ada(c, w, b):
    bsz, d = c.shape
    n = w.shape[1]
    tn = 1536
    return pl.pallas_call(
        _ada_kernel,
        grid=(n // tn,),
        in_specs=[pl.BlockSpec((bsz, d), lambda j: (0, 0)),
                  pl.BlockSpec((d, tn), lambda j: (0, j)),
                  pl.BlockSpec((1, tn), lambda j: (0, j))],
        out_specs=pl.BlockSpec((bsz, tn), lambda j: (0, j)),
        out_shape=jax.ShapeDtypeStruct((bsz, n), f32),
        compiler_params=pltpu.CompilerParams(vmem_limit_bytes=VMEM_LIMIT),
        name="ada",
    )(c, w, b.reshape(1, n))


def _s5_params_kernel(are_ref, aim_ref, ldt_ref, bre_ref, bim_ref,
                      abr_ref, abi_ref, bbr_ref, bbi_ref):
    lre = jnp.minimum(are_ref[...], -1e-4)
    lim = aim_ref[...]
    dt = jnp.exp(ldt_ref[...])
    mag = jnp.exp(lre * dt)
    abr = mag * jnp.cos(lim * dt)
    abi = mag * jnp.sin(lim * dt)
    abr_ref[...] = abr
    abi_ref[...] = abi
    nr = abr - 1.0
    ni = abi
    den = lre * lre + lim * lim
    fr = (nr * lre + ni * lim) / den
    fi = (ni * lre - nr * lim) / den
    bre = bre_ref[...]
    bim = bim_ref[...]
    bbr_ref[...] = fr * bre - fi * bim
    bbi_ref[...] = fr * bim + fi * bre


def _s5_params(a_re, a_im, log_dt, b_re, b_im):
    g, p = a_re.shape
    hg = b_re.shape[2]
    are = a_re.reshape(g, 1, p)
    aim = a_im.reshape(g, 1, p)
    ldt = jnp.broadcast_to(log_dt.reshape(g, 1, 1), (g, 1, p))
    bre = jnp.transpose(b_re, (0, 2, 1))
    bim = jnp.transpose(b_im, (0, 2, 1))
    small = jax.ShapeDtypeStruct((g, 1, p), f32)
    big = jax.ShapeDtypeStruct((g, hg, p), f32)
    return pl.pallas_call(
        _s5_params_kernel,
        out_shape=(small, small, big, big),
        name="s5_params",
    )(are, aim, ldt, bre, bim)


def _mix_in_kernel(x_ref, mod_ref, g_ref, win_ref, lng_ref, lnb_ref, ws_ref, bs_ref,
                   ygm_ref, zs_ref):
    tm = x_ref.shape[0]
    gw = ygm_ref.shape[1]
    x = x_ref[...]
    h = _rms(x, g_ref[...]) * (1.0 + mod_ref[1:2, :]) + mod_ref[0:1, :]
    z = jnp.dot(h.astype(bf16), win_ref[...], preferred_element_type=f32)
    zs_ref[...] = z[:, 2 * gw:]
    u = _gelu(z[:, :gw])
    gv = _gelu(z[:, gw:2 * gw])
    mu = jnp.mean(gv, axis=-1, keepdims=True)
    dv = gv - mu
    var = jnp.mean(dv * dv, axis=-1, keepdims=True)
    v = (dv * lax.rsqrt(var + EPS) * lng_ref[...] + lnb_ref[...]).astype(bf16)
    row = lax.broadcasted_iota(i32, (CHUNK, CHUNK), 0)
    col = lax.broadcasted_iota(i32, (CHUNK, CHUNK), 1)
    causal = row >= col
    hd_w = gw // GM_HEADS
    for hd in range(GM_HEADS):
        wm = jnp.where(causal, ws_ref[hd], 0.0).astype(bf16)
        bias = bs_ref[hd]
        for ck in range(tm // CHUNK):
            rs = slice(ck * CHUNK, (ck + 1) * CHUNK)
            cs = slice(hd * hd_w, (hd + 1) * hd_w)
            mixed = jnp.dot(wm, v[rs, cs], preferred_element_type=f32) + bias
            ygm_ref[rs, cs] = (u[rs, cs] * mixed).astype(bf16)


def _mix_in(x, mod3, g_mix, w_in, ln_g, ln_b, w_s, b_s, tm):
    bsz, seq, d = x.shape
    gw = ln_g.shape[0]
    nin = w_in.shape[1]
    bs_b = jnp.broadcast_to(b_s[:, :, None], (GM_HEADS, CHUNK, gw // GM_HEADS))
    return pl.pallas_call(
        _mix_in_kernel,
        grid=(bsz, seq // tm),
        in_specs=[pl.BlockSpec((None, tm, d), lambda b, i: (b, i, 0)),
                  pl.BlockSpec((None, N_MOD, d), lambda b, i: (b, 0, 0)),
                  pl.BlockSpec((1, d), lambda b, i: (0, 0)),
                  pl.BlockSpec((d, nin), lambda b, i: (0, 0)),
                  pl.BlockSpec((1, gw), lambda b, i: (0, 0)),
                  pl.BlockSpec((1, gw), lambda b, i: (0, 0)),
                  pl.BlockSpec((GM_HEADS, CHUNK, CHUNK), lambda b, i: (0, 0, 0)),
                  pl.BlockSpec((GM_HEADS, CHUNK, gw // GM_HEADS), lambda b, i: (0, 0, 0))],
        out_specs=[pl.BlockSpec((None, tm, gw), lambda b, i: (b, i, 0)),
                   pl.BlockSpec((None, tm, nin - 2 * gw), lambda b, i: (b, i, 0))],
        out_shape=(jax.ShapeDtypeStruct((bsz, seq, gw), bf16),
                   jax.ShapeDtypeStruct((bsz, seq, nin - 2 * gw), f32)),
        compiler_params=pltpu.CompilerParams(
            dimension_semantics=("parallel", "parallel"), vmem_limit_bytes=VMEM_LIMIT),
        name="mix_in",
    )(x, mod3, g_mix.reshape(1, d), w_in.astype(bf16), ln_g.reshape(1, gw), ln_b.reshape(1, gw), w_s, bs_b)


def _s5_kernel(zs_ref, wb_ref, av_ref, wc_ref, dsk_ref, wglu_ref, bglu_ref, o_ref,
               zi, st, carry, yo):
    bsz, tt, w = zs_ref.shape
    nlt = w // LANES
    hw = st.shape[2] // 2
    nhalf = st.shape[0]

    @pl.when(pl.program_id(0) == 0)
    def _():
        carry[...] = jnp.zeros_like(carry)

    for b in range(bsz):
        zb = zs_ref[b]
        for j in range(nlt):
            zi[j, pl.ds(b, tt, stride=bsz), :] = zb[:, j * LANES:(j + 1) * LANES]

    ys = []
    lt_per_half = nlt // nhalf
    for hf in range(nhalf):
        zh = jnp.concatenate([zi[hf * lt_per_half + j] for j in range(lt_per_half)], axis=1)
        st[hf] = jnp.dot(zh.astype(bf16), wb_ref[hf], preferred_element_type=f32)
        ar = jnp.broadcast_to(av_ref[2 * hf:2 * hf + 1, :], (bsz, hw))
        ai = jnp.broadcast_to(av_ref[2 * hf + 1:2 * hf + 2, :], (bsz, hw))

        def step(t, xr, xi):
            r0 = pl.multiple_of(t * bsz, bsz)
            bur = st[hf, pl.ds(r0, bsz), 0:hw]
            bui = st[hf, pl.ds(r0, bsz), hw:2 * hw]
            nxr = ar * xr - ai * xi + bur
            nxi = ar * xi + ai * xr + bui
            st[hf, pl.ds(r0, bsz), 0:hw] = nxr
            st[hf, pl.ds(r0, bsz), hw:2 * hw] = nxi
            return nxr, nxi

        def body(t2, c):
            xr, xi = c
            xr, xi = step(2 * t2, xr, xi)
            xr, xi = step(2 * t2 + 1, xr, xi)
            return xr, xi

        xr, xi = lax.fori_loop(0, tt // 2, body, (carry[hf, :, 0:hw], carry[hf, :, hw:2 * hw]))
        carry[hf, :, 0:hw] = xr
        carry[hf, :, hw:2 * hw] = xi
        ys.append(jnp.dot(st[hf].astype(bf16), wc_ref[hf], preferred_element_type=f32))

    u = jnp.concatenate([zi[j] for j in range(nlt)], axis=1)
    y = jnp.concatenate(ys, axis=1) + dsk_ref[...] * u
    y = _gelu(y)
    gate = jax.nn.sigmoid(jnp.dot(y.astype(bf16), wglu_ref[...], preferred_element_type=f32) + bglu_ref[...])
    y = y * gate
    for j in range(nlt):
        yo[j] = y[:, j * LANES:(j + 1) * LANES]
    for b in range(bsz):
        o_ref[b] = jnp.concatenate(
            [yo[j, pl.ds(b, tt, stride=bsz), :] for j in range(nlt)], axis=1).astype(bf16)


def _s5(zs, abr, abi, bbr, bbi, c_re, c_im, d_skip, w_glu, b_glu, tt):
    bsz, seq, w = zs.shape
    g, hg, p = bbr.shape
    nhalf = 2
    gh = g // nhalf
    eye = jnp.eye(gh, dtype=f32)

    def blockdiag_in(m):
        return jnp.einsum('ghp,gk->ghkp', m, eye).reshape(gh * hg, gh * p)

    def blockdiag_out(m):
        return jnp.einsum('ghp,gk->gpkh', m, eye).reshape(gh * p, gh * hg)

    wb = jnp.stack([jnp.concatenate([blockdiag_in(bbr[h * gh:(h + 1) * gh]),
                                     blockdiag_in(bbi[h * gh:(h + 1) * gh])], axis=1)
                    for h in range(nhalf)]).astype(bf16)
    wc = jnp.stack([jnp.concatenate([blockdiag_out(c_re[h * gh:(h + 1) * gh]),
                                     blockdiag_out(-c_im[h * gh:(h + 1) * gh])], axis=0)
                    for h in range(nhalf)]).astype(bf16)
    av = jnp.concatenate([jnp.stack([abr[h * gh:(h + 1) * gh].reshape(gh * p),
                                     abi[h * gh:(h + 1) * gh].reshape(gh * p)])
                          for h in range(nhalf)], axis=0)
    hw = gh * p
    nlt = w // LANES
    return pl.pallas_call(
        _s5_kernel,
        grid=(seq // tt,),
        in_specs=[pl.BlockSpec((bsz, tt, w), lambda i: (0, i, 0)),
                  pl.BlockSpec(wb.shape, lambda i: (0, 0, 0)),
                  pl.BlockSpec(av.shape, lambda i: (0, 0)),
                  pl.BlockSpec(wc.shape, lambda i: (0, 0, 0)),
                  pl.BlockSpec((1, w), lambda i: (0, 0)),
                  pl.BlockSpec((w, w), lambda i: (0, 0)),
                  pl.BlockSpec((1, w), lambda i: (0, 0))],
        out_specs=pl.BlockSpec((bsz, tt, w), lambda i: (0, i, 0)),
        out_shape=jax.ShapeDtypeStruct((bsz, seq, w), bf16),
        scratch_shapes=[pltpu.VMEM((nlt, tt * bsz, LANES), f32),
                        pltpu.VMEM((nhalf, tt * bsz, 2 * hw), f32),
                        pltpu.VMEM((nhalf, bsz, 2 * hw), f32),
                        pltpu.VMEM((nlt, tt * bsz, LANES), f32)],
        compiler_params=pltpu.CompilerParams(
            dimension_semantics=("arbitrary",), vmem_limit_bytes=VMEM_LIMIT),
        name="s5",
    )(zs, wb, av, wc, d_skip.reshape(1, w), w_glu.astype(bf16), b_glu.reshape(1, w))


def _sort16_pairs():
    n, pairs, p = 16, [], 1
    while p < n:
        k = p
        while k >= 1:
            for j in range(k % p, n - k, 2 * k):
                for i in range(min(k, n - j - k)):
                    if (i + j) // (2 * p) == (i + j + k) // (2 * p):
                        pairs.append((i + j, i + j + k))
            k //= 2
        p *= 2
    return pairs


def _top16(vals):
    ng = vals.shape[0] // SUBLANES
    assert ng == 16
    v = [vals[g * SUBLANES:(g + 1) * SUBLANES] for g in range(ng)]
    sub = lax.broadcasted_iota(i32, (SUBLANES, vals.shape[1]), 0).astype(f32)
    ix = [sub + float(SUBLANES * g) for g in range(ng)]
    for a, b in _sort16_pairs():
        keep = (v[a] > v[b]) | ((v[a] == v[b]) & (ix[a] < ix[b]))
        v[a], v[b] = jnp.where(keep, v[a], v[b]), jnp.where(keep, v[b], v[a])
        ix[a], ix[b] = jnp.where(keep, ix[a], ix[b]), jnp.where(keep, ix[b], ix[a])
    tops, topi = [], []
    for r in range(PEER_TOPK):
        m = jnp.max(v[0], axis=0, keepdims=True)
        am = jnp.min(jnp.where(v[0] == m, ix[0], float(vals.shape[0])), axis=0, keepdims=True)
        tops.append(m)
        topi.append(am)
        win = ix[0] == am
        for dpt in range(PEER_TOPK - 1 - r):
            v[dpt] = jnp.where(win, v[dpt + 1], v[dpt])
            ix[dpt] = jnp.where(win, ix[dpt + 1], ix[dpt])
    return tops, topi


def _route_head(keys2, qt2):
    tm = qt2.shape[1]
    ncand = 16 + 7 * 8 + 8
    prow = lax.broadcasted_iota(i32, (ncand, tm), 0)
    pos = jnp.where(prow < 16, prow,
                    jnp.where(prow < 72, jnp.right_shift(prow - 8, 3) * 16 + jnp.bitwise_and(prow, 7),
                              (prow - 64) * 16)).astype(f32)
    halves = []
    for c in range(2):
        s = jnp.dot(keys2[c], qt2[c * PEER_DHALF:(c + 1) * PEER_DHALF, :], preferred_element_type=f32)
        halves.append(_top16(s))
    (ta, ia), (tb, ib) = halves
    b16 = jnp.concatenate(tb, axis=0)
    ib16 = jnp.concatenate(ib, axis=0)
    b8, ib8 = b16[:8], ib16[:8]
    cand = [ta[0] + b16]
    cid = [ia[0] * float(PEER_NKEYS) + ib16]
    for i in range(1, 8):
        cand.append(ta[i] + b8)
        cid.append(ia[i] * float(PEER_NKEYS) + ib8)
    a_hi = jnp.concatenate(ta[8:], axis=0)
    ia_hi = jnp.concatenate(ia[8:], axis=0)
    cand.append(a_hi + tb[0])
    cid.append(ia_hi * float(PEER_NKEYS) + ib[0])
    cand = jnp.concatenate(cand, axis=0)
    nexp = float(PEER_NKEYS * PEER_NKEYS)
    key = pos * nexp + jnp.concatenate(cid, axis=0)
    best, eid = [], []
    for _ in range(PEER_TOPK):
        m = jnp.max(cand, axis=0, keepdims=True)
        kmin = jnp.min(jnp.where(cand == m, key, 1e9), axis=0, keepdims=True)
        eid.append(kmin - jnp.floor(kmin * (1.0 / nexp)) * nexp)
        best.append(m)
        cand = jnp.where(key == kmin, -jnp.inf, cand)
    best = jnp.concatenate(best, axis=0)
    e = jnp.exp(best - best[0:1])
    return e / jnp.sum(e, axis=0, keepdims=True), jnp.concatenate(eid, axis=0)


def _store_routes(gate_t, id_t, idx_ref, gate_ref):
    gate_ref[...] = gate_t.T
    idx_ref[...] = (id_t.T * 4.0).astype(i32)


def _mix_out_kernel(x_ref, ygm_ref, yss_ref, mod_ref, wo1_ref, wo2_ref, g_ref, wqt_ref, keys_ref,
                    x1_ref, h2_ref, idx_ref, gate_ref):
    tm, d = x_ref.shape
    y = jnp.dot(ygm_ref[...], wo1_ref[...], preferred_element_type=f32)
    y = y + jnp.dot(yss_ref[...], wo2_ref[...], preferred_element_type=f32)
    x1 = x_ref[...] + mod_ref[2:3, :] * y
    x1_ref[...] = x1
    h2 = _rms(x1, g_ref[...]) * (1.0 + mod_ref[4:5, :]) + mod_ref[3:4, :]
    for c in range(d // LANES):
        h2_ref[pl.ds(c, tm, stride=d // LANES), :] = h2[:, c * LANES:(c + 1) * LANES]
    qt = lax.dot_general(wqt_ref[...], h2.astype(bf16), _NT, preferred_element_type=f32).astype(bf16)
    dk = 2 * PEER_DHALF
    routes = [_route_head(keys_ref[hd], qt[hd * dk:(hd + 1) * dk, :]) for hd in range(PEER_HEADS)]
    _store_routes(jnp.concatenate([g for g, _ in routes], axis=0),
                  jnp.concatenate([e for _, e in routes], axis=0), idx_ref, gate_ref)


def _mix_out(x, ygm, yss, mod3, w_out, g_ffn, w_q, keys, tm):
    bsz, seq, d = x.shape
    gw = ygm.shape[2]
    nt = seq // tm
    n = bsz * seq
    dq = w_q.shape[1]
    nk = PEER_HEADS * PEER_TOPK
    wo = w_out.astype(bf16)
    return pl.pallas_call(
        _mix_out_kernel,
        grid=(bsz, nt),
        in_specs=[pl.BlockSpec((None, tm, d), lambda b, i: (b, i, 0)),
                  pl.BlockSpec((None, tm, gw), lambda b, i: (b, i, 0)),
                  pl.BlockSpec((None, tm, gw), lambda b, i: (b, i, 0)),
                  pl.BlockSpec((None, N_MOD, d), lambda b, i: (b, 0, 0)),
                  pl.BlockSpec((gw, d), lambda b, i: (0, 0)),
                  pl.BlockSpec((gw, d), lambda b, i: (0, 0)),
                  pl.BlockSpec((1, d), lambda b, i: (0, 0)),
                  pl.BlockSpec((dq, d), lambda b, i: (0, 0)),
                  pl.BlockSpec(keys.shape, lambda b, i: (0, 0, 0, 0))],
        out_specs=[pl.BlockSpec((tm, d), lambda b, i: (b * nt + i, 0)),
                   pl.BlockSpec((tm * (d // LANES), LANES), lambda b, i: (b * nt + i, 0)),
                   pl.BlockSpec((tm, nk), lambda b, i: (b * nt + i, 0)),
                   pl.BlockSpec((tm, nk), lambda b, i: (b * nt + i, 0))],
        out_shape=(jax.ShapeDtypeStruct((n, d), f32),
                   jax.ShapeDtypeStruct((n * (d // LANES), LANES), f32),
                   jax.ShapeDtypeStruct((n, nk), i32),
                   jax.ShapeDtypeStruct((n, nk), f32)),
        compiler_params=pltpu.CompilerParams(
            dimension_semantics=("parallel", "parallel"), vmem_limit_bytes=VMEM_LIMIT),
        name="mix_out",
    )(x, ygm, yss, mod3, wo[:gw], wo[gw:], g_ffn.reshape(1, d), w_q.T.astype(bf16), keys.astype(bf16))


def _pack_table_kernel(t_ref, o_ref):
    eb, d = t_ref.shape
    rows = d // (2 * LANES)
    x = t_ref[...].astype(bf16).astype(f32)
    bits = lax.bitcast_convert_type(x, i32)
    for s in range(rows):
        lo = lax.shift_right_logical(bits[:, 2 * s * LANES:(2 * s + 1) * LANES], 16)
        hi = jnp.bitwise_and(bits[:, (2 * s + 1) * LANES:(2 * s + 2) * LANES], -65536)
        o_ref[pl.ds(s, eb, stride=rows), :] = jnp.bitwise_or(lo, hi)


def _pack_table(tab):
    e, d = tab.shape
    rows = d // (2 * LANES)
    eb = 512
    return pl.pallas_call(
        _pack_table_kernel,
        grid=(e // eb,),
        in_specs=[pl.BlockSpec((eb, d), lambda i: (i, 0))],
        out_specs=pl.BlockSpec((eb * rows, LANES), lambda i: (i, 0)),
        out_shape=jax.ShapeDtypeStruct((e * rows, LANES), i32),
        compiler_params=pltpu.CompilerParams(
            dimension_semantics=("parallel",), vmem_limit_bytes=VMEM_LIMIT),
        name="pack_table",
    )(tab)


def _gather_rows(idx_ref, ts, tab_ref, nk, rows):
    row_refs = [idx_ref.at[pl.ds(t, 1)] for t in ts]
    out = []
    for row_ref in row_refs:
        slabs = [tab_ref[pl.ds(pl.multiple_of(row_ref[0, k], rows), rows), :] for k in range(nk)]
        out.append(pltpu.bitcast(jnp.concatenate(slabs, axis=0), bf16))
    return out


def _split_bf16(x):
    hi = x.astype(bf16)
    return hi, (x - hi.astype(f32)).astype(bf16)


def _peer_u_kernel(idx_ref, h_ref, gate_ref, tab_ref, w_ref, res_all):
    tt, nk = gate_ref.shape
    nch = h_ref.shape[1]
    rows = nch // 2
    d = nch * LANES
    sh = int(math.log2(nch))
    nun = PEER_UNROLL

    def body(i, carry):
        ts = [i * nun + u for u in range(nun)]
        ms = _gather_rows(idx_ref, ts, tab_ref, nk, rows)
        for t, m in zip(ts, ms):
            hb = h_ref[t].astype(bf16)
            r0 = pl.multiple_of(t * nch, nch)
            res_all[pl.ds(r0, nch), :] = lax.dot_general(hb, m, _NT, preferred_element_type=f32)
        return carry

    lax.fori_loop(0, tt // nun, body, 0)

    c_i = lax.broadcasted_iota(i32, (nch, d), 0)
    l_i = lax.broadcasted_iota(i32, (nch, d), 1)
    diag = (jnp.bitwise_and(l_i, nch - 1) == c_i).astype(f32)
    masked = (res_all[...].reshape(tt, nch, d) * diag).reshape(tt * nch, d)
    grp = (jnp.right_shift(lax.broadcasted_iota(i32, (d, nk), 0), sh)
           == lax.broadcasted_iota(i32, (d, nk), 1)).astype(f32).astype(bf16)
    mhi, mlo = _split_bf16(masked)
    g = jnp.dot(mhi, grp, preferred_element_type=f32) + jnp.dot(mlo, grp, preferred_element_type=f32)
    sel = (jnp.right_shift(lax.broadcasted_iota(i32, (tt, tt * nch), 1), sh)
           == lax.broadcasted_iota(i32, (tt, tt * nch), 0)).astype(f32).astype(bf16)
    ghi, glo = _split_bf16(g)
    act = jnp.dot(sel, ghi, preferred_element_type=f32) + jnp.dot(sel, glo, preferred_element_type=f32)
    w_ref[...] = gate_ref[...] * _gelu(act)


def _peer_v_kernel(idx_ref, w_ref, tab_ref, o_ref, whi, wlo):
    tt, nk = w_ref.shape
    nch = o_ref.shape[1]
    rows = nch // 2
    d = nch * LANES
    sh = int(math.log2(nch))
    nun = PEER_UNROLL
    k_i = lax.broadcasted_iota(i32, (nk, d), 0)
    n_i = lax.broadcasted_iota(i32, (nk, d), 1)
    expand = (jnp.right_shift(n_i, sh) == k_i).astype(f32).astype(bf16)
    hi, lo = _split_bf16(w_ref[...])
    whi[...] = jnp.dot(hi, expand, preferred_element_type=f32)
    wlo[...] = jnp.dot(lo, expand, preferred_element_type=f32)
    c_i = lax.broadcasted_iota(i32, (nch, d), 0)
    l_i = lax.broadcasted_iota(i32, (nch, d), 1)
    diag = (jnp.bitwise_and(l_i, nch - 1) == c_i).astype(f32)

    def body(i, carry):
        ts = [i * nun + u for u in range(nun)]
        ms = _gather_rows(idx_ref, ts, tab_ref, nk, rows)
        for t, m in zip(ts, ms):
            lhs = jnp.concatenate([whi[pl.ds(t, 1), :] * diag, wlo[pl.ds(t, 1), :] * diag], axis=0).astype(bf16)
            out = jnp.dot(lhs, m, preferred_element_type=f32)
            o_ref[t] = out[:nch] + out[nch:]
        return carry

    lax.fori_loop(0, tt // nun, body, 0)


def _peer_u(idx, h2r, gate, tab, tt):
    n, nk = gate.shape
    nch = h2r.shape[0] // n
    h3 = h2r.reshape(n, nch, LANES)
    return pl.pallas_call(
        _peer_u_kernel,
        grid=(n // tt,),
        in_specs=[pl.BlockSpec((tt, nk), lambda i: (i, 0), memory_space=pltpu.SMEM),
                  pl.BlockSpec((tt, nch, LANES), lambda i: (i, 0, 0)),
                  pl.BlockSpec((tt, nk), lambda i: (i, 0)),
                  pl.BlockSpec(tab.shape, lambda i: (0, 0), pipeline_mode=pl.Buffered(1))],
        out_specs=pl.BlockSpec((tt, nk), lambda i: (i, 0)),
        out_shape=jax.ShapeDtypeStruct((n, nk), f32),
        scratch_shapes=[pltpu.VMEM((tt * nch, nch * LANES), f32)],
        compiler_params=pltpu.CompilerParams(
            dimension_semantics=("parallel",), vmem_limit_bytes=VMEM_LIMIT),
        name="peer_u",
    )(idx, h3, gate, tab)


def _peer_v(idx, w, tab, nch, tt):
    n, nk = w.shape
    d = nch * LANES
    return pl.pallas_call(
        _peer_v_kernel,
        grid=(n // tt,),
        in_specs=[pl.BlockSpec((tt, nk), lambda i: (i, 0), memory_space=pltpu.SMEM),
                  pl.BlockSpec((tt, nk), lambda i: (i, 0)),
                  pl.BlockSpec(tab.shape, lambda i: (0, 0), pipeline_mode=pl.Buffered(1))],
        out_specs=pl.BlockSpec((tt, nch, LANES), lambda i: (i, 0, 0)),
        out_shape=jax.ShapeDtypeStruct((n, nch, LANES), f32),
        scratch_shapes=[pltpu.VMEM((tt, d), f32), pltpu.VMEM((tt, d), f32)],
        compiler_params=pltpu.CompilerParams(
            dimension_semantics=("parallel",), vmem_limit_bytes=VMEM_LIMIT),
        name="peer_v",
    )(idx, w, tab)


def _final_kernel(x1_ref, p_ref, mod_ref, g_ref, o_ref):
    tm, d = x1_ref.shape
    nch = d // LANES
    peer = jnp.concatenate([p_ref[pl.ds(c, tm, stride=nch), :] for c in range(nch)], axis=1)
    x2 = x1_ref[...] + mod_ref[5:6, :] * peer
    o_ref[...] = _rms(x2, g_ref[...])


def _final(x1, peer2d, mod3, g_final, bsz, seq, tm):
    n, d = x1.shape
    nt = seq // tm
    nch = d // LANES
    return pl.pallas_call(
        _final_kernel,
        grid=(bsz, nt),
        in_specs=[pl.BlockSpec((tm, d), lambda b, i: (b * nt + i, 0)),
                  pl.BlockSpec((tm * nch, LANES), lambda b, i: (b * nt + i, 0)),
                  pl.BlockSpec((None, N_MOD, d), lambda b, i: (b, 0, 0)),
                  pl.BlockSpec((1, d), lambda b, i: (0, 0))],
        out_specs=pl.BlockSpec((None, tm, d), lambda b, i: (b, i, 0)),
        out_shape=jax.ShapeDtypeStruct((bsz, seq, d), f32),
        compiler_params=pltpu.CompilerParams(
            dimension_semantics=("parallel", "parallel"), vmem_limit_bytes=VMEM_LIMIT),
        name="final",
    )(x1, peer2d, mod3, g_final.reshape(1, d))


def kernel(x, c, w_ada, b_ada, g_mix, w_in, sgu_ln_g, sgu_ln_b, w_s, b_s, ssm_a_re, ssm_a_im, ssm_log_dt, ssm_b_re, ssm_b_im, ssm_c_re, ssm_c_im, ssm_d, w_glu, b_glu, w_out, g_ffn, w_q, peer_keys, peer_u, peer_v, g_final):
    bsz, seq, d = x.shape
    depth = w_ada.shape[0]
    assert depth == 1 and d % LANES == 0 and seq % CHUNK == 0
    tm_in = min(seq, 512)
    tm_out = min(seq, 512)
    tt_scan = min(seq, 64)
    tt_peer = 128
    n = bsz * seq
    nch = d // LANES

    l = 0
    mod3 = _ada(c, w_ada[l], b_ada[l]).reshape(bsz, N_MOD, d)
    ygm, zs = _mix_in(x, mod3, g_mix[l], w_in[l], sgu_ln_g[l], sgu_ln_b[l], w_s[l], b_s[l], tm_in)
    abr, abi, bbr, bbi = _s5_params(ssm_a_re[l], ssm_a_im[l], ssm_log_dt[l], ssm_b_re[l], ssm_b_im[l])
    yss = _s5(zs, abr[:, 0, :], abi[:, 0, :], bbr, bbi, ssm_c_re[l], ssm_c_im[l],
              ssm_d[l], w_glu[l], b_glu[l], tt_scan)
    x1, h2r, idx, gate = _mix_out(x, ygm, yss, mod3, w_out[l], g_ffn[l], w_q[l], peer_keys[l], tm_out)
    w = _peer_u(idx, h2r, gate, _pack_table(peer_u[l]), tt_peer)
    peer = _peer_v(idx, w, _pack_table(peer_v[l]), nch, tt_peer)
    return _final(x1, peer.reshape(n * nch, LANES), mod3, g_final, bsz, seq, tm_in)
```
